```python
import math
import jax, jax.numpy as jnp
from jax import lax
import numpy as np

D_MODEL = 4096
BATCH = 4
SEQ = 2048
DEPTH = 1
DEC_BATCH = 128
DEC_SEQ = 1
PAST_LEN = 8192
PAGE_SIZE = 128

MIX_WIDTH = D_MODEL
ATTN_WIDTH = MIX_WIDTH // 2
POOL_WIDTH = MIX_WIDTH - ATTN_WIDTH
HEAD_DIM = 64
N_HEADS = ATTN_WIDTH // HEAD_DIM
N_KV_HEADS = N_HEADS // 8
GROUP = N_HEADS // N_KV_HEADS
KV_WIDTH = N_KV_HEADS * HEAD_DIM
WINDOW = 128
BLOCK = 128
POOL_WINDOWS = (2, 4, 8, 16)
N_POOL_GROUPS = len(POOL_WINDOWS)
POOL_GROUP_WIDTH = POOL_WIDTH // N_POOL_GROUPS
POOL_STATE_ROWS = max(POOL_WINDOWS) - 1
D_FF = ((8 * D_MODEL + 3 * 256 - 1) // (3 * 256)) * 256
QKVU_WIDTH = ATTN_WIDTH + 2 * KV_WIDTH + POOL_WIDTH
NORM_EPS = 1e-5
NEG_INF = -1e30

kernel_name = "hymba_pool_swa_sink_decoder_step"


def rmsnorm(x, g):
    xf = x.astype(jnp.float32)
    xf = xf * lax.rsqrt(jnp.mean(xf * xf, axis=-1, keepdims=True) + NORM_EPS)
    return (xf * g.astype(jnp.float32)).astype(x.dtype)


def split_proj(h, w_in):
    B, L, _ = h.shape
    p = jnp.einsum("bld,de->ble", h, w_in)
    q = p[..., :ATTN_WIDTH].reshape(B, L, N_KV_HEADS, GROUP, HEAD_DIM)
    k = p[..., ATTN_WIDTH:ATTN_WIDTH + KV_WIDTH].reshape(B, L, N_KV_HEADS, HEAD_DIM)
    v = p[..., ATTN_WIDTH + KV_WIDTH:ATTN_WIDTH + 2 * KV_WIDTH].reshape(B, L, N_KV_HEADS, HEAD_DIM)
    u = p[..., ATTN_WIDTH + 2 * KV_WIDTH:]
    return q, k, v, u


def sink_softmax(s, mask, sink):
    s = jnp.where(mask, s.astype(jnp.float32), NEG_INF)
    sink = sink.astype(jnp.float32)
    m = jnp.maximum(jnp.max(s, axis=-1, keepdims=True), sink)
    e = jnp.exp(s - m)
    return e / (jnp.sum(e, axis=-1, keepdims=True) + jnp.exp(sink - m))


def swa_prompt(q, k, v, sinks):
    B, S = q.shape[:2]
    nb = S // BLOCK
    qb = q.reshape(B, nb, BLOCK, N_KV_HEADS, GROUP, HEAD_DIM)
    kb = k.reshape(B, nb, BLOCK, N_KV_HEADS, HEAD_DIM)
    vb = v.reshape(B, nb, BLOCK, N_KV_HEADS, HEAD_DIM)
    pad = ((0, 0), (1, 0), (0, 0), (0, 0), (0, 0))
    kk = jnp.concatenate([jnp.pad(kb, pad)[:, :nb], kb], axis=2)
    vv = jnp.concatenate([jnp.pad(vb, pad)[:, :nb], vb], axis=2)
    s = jnp.einsum("bnqkgd,bnskd->bnkgqs", qb, kk) * (HEAD_DIM ** -0.5)
    qi = jnp.arange(BLOCK)[:, None]
    kj = jnp.arange(2 * BLOCK)[None, :] - BLOCK
    band = (kj <= qi) & (qi - kj < WINDOW)
    valid = (jnp.arange(nb)[:, None, None] * BLOCK + kj[None]) >= 0
    mask = (band[None] & valid)[None, :, None, None]
    p = sink_softmax(s, mask, sinks.reshape(1, 1, N_KV_HEADS, GROUP, 1, 1))
    o = jnp.einsum("bnkgqs,bnskd->bnqkgd", p.astype(vv.dtype), vv)
    return o.reshape(B, S, ATTN_WIDTH)


def swa_decode(q, k_new, v_new, cache_k, cache_v, sinks):
    B, T = q.shape[:2]
    W = cache_k.shape[1]
    kk = jnp.concatenate([cache_k, k_new], axis=1)
    vv = jnp.concatenate([cache_v, v_new], axis=1)
    pos_q = PAST_LEN + jnp.arange(T)
    pos_k = PAST_LEN - W + jnp.arange(W + T)
    mask = (pos_k[None, :] <= pos_q[:, None]) & (pos_q[:, None] - pos_k[None, :] < WINDOW)
    s = jnp.einsum("bqkgd,bskd->bkgqs", q, kk) * (HEAD_DIM ** -0.5)
    p = sink_softmax(s, mask[None, None, None], sinks.reshape(1, N_KV_HEADS, GROUP, 1, 1))
    o = jnp.einsum("bkgqs,bskd->bqkgd", p.astype(vv.dtype), vv)
    return o.reshape(B, T, ATTN_WIDTH), kk[:, -W:], vv[:, -W:]


def pool_mix(u, pos, w_pool, pool_scale):
    B, L, _ = u.shape
    ug = u.reshape(B, L, N_POOL_GROUPS, POOL_GROUP_WIDTH).astype(jnp.float32)
    outs = []
    for g, w in enumerate(POOL_WINDOWS):
        x = ug[:, :, g]
        cs = jnp.cumsum(x, axis=1)
        shifted = jnp.pad(cs, ((0, 0), (w, 0), (0, 0)))[:, :L]
        cnt = jnp.minimum(pos + 1, w).astype(jnp.float32)[None, :, None]
        outs.append((cs - shifted) / cnt - x)
    d = jnp.stack(outs, axis=2).astype(u.dtype)
    y = jnp.einsum("blgc,gce->blge", d, w_pool).reshape(B, L, POOL_WIDTH)
    return y * pool_scale


def post_mix(x, a, pooled, w_o, g_ffn, w_gate, w_up, w_down):
    x = x + jnp.einsum("ble,ed->bld", jnp.concatenate([a, pooled], axis=-1), w_o)
    h = rmsnorm(x, g_ffn)
    ff = jax.nn.silu(jnp.einsum("bld,df->blf", h, w_gate)) * jnp.einsum("bld,df->blf", h, w_up)
    return x + jnp.einsum("blf,fd->bld", ff, w_down)


def setup_inputs(seed: int = 0) -> dict:
    key = jax.random.key(seed)
    ks = jax.random.split(key, 20)
    f32 = jnp.float32
    n = lambda k, shape, scale: (jax.random.normal(k, shape, f32) * scale)
    w_rows = min(WINDOW, PAST_LEN)
    return {
        "x_prompt": n(ks[0], (BATCH, SEQ, D_MODEL), 1.0),
        "x_sample": n(ks[1], (DEC_BATCH, DEC_SEQ, D_MODEL), 1.0),
        "cache_k_win": n(ks[2], (DEPTH, DEC_BATCH, w_rows, N_KV_HEADS, HEAD_DIM), 1.0),
        "cache_v_win": n(ks[3], (DEPTH, DEC_BATCH, w_rows, N_KV_HEADS, HEAD_DIM), 1.0),
        "state_pool": n(ks[4], (DEPTH, DEC_BATCH, POOL_STATE_ROWS, POOL_WIDTH), 1.0),
        "g_mix": 1.0 + n(ks[5], (DEPTH, D_MODEL), 0.02),
        "w_in": n(ks[6], (DEPTH, D_MODEL, QKVU_WIDTH), D_MODEL ** -0.5),
        "sinks": n(ks[7], (DEPTH, N_HEADS), 0.5),
        "w_pool": n(ks[8], (DEPTH, N_POOL_GROUPS, POOL_GROUP_WIDTH, POOL_GROUP_WIDTH), POOL_GROUP_WIDTH ** -0.5),
        "pool_scale": 1.0 + n(ks[9], (DEPTH, POOL_WIDTH), 0.02),
        "w_o": n(ks[10], (DEPTH, MIX_WIDTH, D_MODEL), MIX_WIDTH ** -0.5),
        "g_ffn": 1.0 + n(ks[11], (DEPTH, D_MODEL), 0.02),
        "w_gate": n(ks[12], (DEPTH, D_MODEL, D_FF), D_MODEL ** -0.5),
        "w_up": n(ks[13], (DEPTH, D_MODEL, D_FF), D_MODEL ** -0.5),
        "w_down": n(ks[14], (DEPTH, D_FF, D_MODEL), D_FF ** -0.5),
        "g_final": 1.0 + n(ks[15], (D_MODEL,), 0.02),
    }


def reference(x_prompt, x_sample, cache_k_win, cache_v_win, state_pool, g_mix, w_in, sinks,
              w_pool, pool_scale, w_o, g_ffn, w_gate, w_up, w_down, g_final):
    S = x_prompt.shape[1]
    T = x_sample.shape[1]
    pw = min(WINDOW, S)
    pos_prompt = jnp.arange(S)
    pos_sample_ext = PAST_LEN - POOL_STATE_ROWS + jnp.arange(POOL_STATE_ROWS + T)
    xp, xs = x_prompt, x_sample
    kp_l, vp_l, pp_l, ks_l, vs_l, ps_l = [], [], [], [], [], []
    for l in range(DEPTH):
        q, k, v, u = split_proj(rmsnorm(xp, g_mix[l]), w_in[l])
        a = swa_prompt(q, k, v, sinks[l])
        pooled = pool_mix(u, pos_prompt, w_pool[l], pool_scale[l])
        xp = post_mix(xp, a, pooled, w_o[l], g_ffn[l], w_gate[l], w_up[l], w_down[l])
        kp_l.append(k[:, -pw:])
        vp_l.append(v[:, -pw:])
        pp_l.append(u[:, -POOL_STATE_ROWS:])
        q, k, v, u = split_proj(rmsnorm(xs, g_mix[l]), w_in[l])
        a, k_win, v_win = swa_decode(q, k, v, cache_k_win[l], cache_v_win[l], sinks[l])
        u_ext = jnp.concatenate([state_pool[l], u], axis=1)
        pooled = pool_mix(u_ext, pos_sample_ext, w_pool[l], pool_scale[l])[:, POOL_STATE_ROWS:]
        xs = post_mix(xs, a, pooled, w_o[l], g_ffn[l], w_gate[l], w_up[l], w_down[l])
        ks_l.append(k_win)
        vs_l.append(v_win)
        ps_l.append(u_ext[:, -POOL_STATE_ROWS:])
    y_prompt = rmsnorm(xp, g_final)
    y_sample = rmsnorm(xs, g_final)
    return (y_prompt, y_sample, jnp.stack(kp_l), jnp.stack(vp_l), jnp.stack(pp_l),
            jnp.stack(ks_l), jnp.stack(vs_l), jnp.stack(ps_l))
```

```python
import functools

import jax
import jax.numpy as jnp
import numpy as np
from jax import lax
from jax.experimental import pallas as pl
from jax.experimental.pallas import tpu as pltpu

HEAD_DIM = 64
N_KV_HEADS = 4
GROUP = 8
N_HEADS = N_KV_HEADS * GROUP
ATTN_WIDTH = N_HEADS * HEAD_DIM
KV_WIDTH = N_KV_HEADS * HEAD_DIM
WINDOW = 128
BLOCK = 128
PAST_LEN = 8192
POOL_WINDOWS = (2, 4, 8, 16)
POOL_STATE_ROWS = max(POOL_WINDOWS) - 1
NORM_EPS = 1e-5
NEG_INF = -1e30

BF16 = jnp.bfloat16
F32 = jnp.float32
MIB = 1024 * 1024


def _rmsnorm(xf, g):
    return xf * lax.rsqrt(jnp.mean(xf * xf, axis=-1, keepdims=True) + NORM_EPS) * g


def _params(semantics, vmem_mib):
    return pltpu.CompilerParams(dimension_semantics=semantics, vmem_limit_bytes=vmem_mib * MIB)


def _proj_kernel(x_ref, g_ref, w_ref, p_ref, h_ref):
    @pl.when(pl.program_id(1) == 0)
    def _():
        h_ref[...] = _rmsnorm(x_ref[...], g_ref[...]).astype(BF16)

    p_ref[...] = jnp.dot(h_ref[...], w_ref[...], preferred_element_type=F32)


def _proj(x, g, w, *, tm, tn):
    m, d = x.shape
    n = w.shape[1]
    return pl.pallas_call(
        _proj_kernel,
        out_shape=jax.ShapeDtypeStruct((m, n), F32),
        grid=(m // tm, n // tn),
        in_specs=[
            pl.BlockSpec((tm, d), lambda i, j: (i, 0)),
            pl.BlockSpec((1, d), lambda i, j: (0, 0)),
            pl.BlockSpec((d, tn), lambda i, j: (0, j)),
        ],
        out_specs=pl.BlockSpec((tm, tn), lambda i, j: (i, j)),
        scratch_shapes=[pltpu.VMEM((tm, d), BF16)],
        compiler_params=_params(("parallel", "arbitrary"), 48),
        name="proj",
    )(x, g, w)


def _sink_softmax(s, sink):
    m = jnp.maximum(jnp.max(s, axis=-1, keepdims=True), sink)
    e = jnp.exp(s - m)
    den = jnp.sum(e, axis=-1, keepdims=True) + jnp.exp(sink - m)
    return e * (1.0 / den)


def _attn_prompt_kernel(sinks_ref, q_ref, kvp_ref, kvc_ref, o_ref):
    nblk = pl.program_id(1)
    qi = lax.broadcasted_iota(jnp.int32, (BLOCK, 2 * BLOCK), 0)
    kj = lax.broadcasted_iota(jnp.int32, (BLOCK, 2 * BLOCK), 1) - BLOCK
    mask = (kj <= qi) & (qi - kj < WINDOW) & (nblk * BLOCK + kj >= 0)
    scale = HEAD_DIM ** -0.5
    for kh in range(N_KV_HEADS):
        ks = slice(kh * HEAD_DIM, (kh + 1) * HEAD_DIM)
        vs = slice(KV_WIDTH + kh * HEAD_DIM, KV_WIDTH + (kh + 1) * HEAD_DIM)
        kcat = jnp.concatenate([kvp_ref[:, ks], kvc_ref[:, ks]], axis=0).astype(BF16)
        vcat = jnp.concatenate([kvp_ref[:, vs], kvc_ref[:, vs]], axis=0).astype(BF16)
        for gp in range(GROUP // 2):
            pair = []
            for gi in range(2):
                h = kh * GROUP + gp * 2 + gi
                qh = (q_ref[:, h * HEAD_DIM:(h + 1) * HEAD_DIM] * scale).astype(BF16)
                s = lax.dot_general(qh, kcat, (((1,), (1,)), ((), ())), preferred_element_type=F32)
                s = jnp.where(mask, s, NEG_INF)
                p = _sink_softmax(s, sinks_ref[h])
                pair.append(jnp.dot(p.astype(BF16), vcat, preferred_element_type=F32))
            c0 = (kh * GROUP + gp * 2) * HEAD_DIM
            o_ref[:, c0:c0 + 2 * HEAD_DIM] = jnp.concatenate(pair, axis=1).astype(o_ref.dtype)


def _attn_prompt(p3, sinks):
    b, s, _ = p3.shape
    nb = s // BLOCK
    kv_blk = ATTN_WIDTH // (2 * KV_WIDTH)
    return pl.pallas_call(
        _attn_prompt_kernel,
        out_shape=jax.ShapeDtypeStruct((b, s, ATTN_WIDTH), BF16),
        grid=(b, nb),
        in_specs=[
            pl.BlockSpec(memory_space=pltpu.SMEM),
            pl.BlockSpec((None, BLOCK, ATTN_WIDTH), lambda i, n: (i, n, 0)),
            pl.BlockSpec((None, BLOCK, 2 * KV_WIDTH), lambda i, n: (i, jnp.maximum(n - 1, 0), kv_blk)),
            pl.BlockSpec((None, BLOCK, 2 * KV_WIDTH), lambda i, n: (i, n, kv_blk)),
        ],
        out_specs=pl.BlockSpec((None, BLOCK, ATTN_WIDTH), lambda i, n: (i, n, 0)),
        compiler_params=_params(("parallel", "parallel"), 32),
        name="attn_prompt",
    )(sinks, p3, p3, p3)


def _attn_decode_kernel(sinks_ref, q_ref, k_ref, v_ref, o_ref):
    nb = q_ref.shape[0]
    scale = HEAD_DIM ** -0.5

    def body(i, carry):
        kb = k_ref[i].astype(BF16)
        vb = v_ref[i].astype(BF16)
        qb = (q_ref[i] * scale).astype(BF16)
        outs = []
        for kh in range(N_KV_HEADS):
            hs = slice(kh * GROUP, (kh + 1) * GROUP)
            cs = slice(kh * HEAD_DIM, (kh + 1) * HEAD_DIM)
            s = lax.dot_general(qb[hs], kb[:, cs], (((1,), (1,)), ((), ())), preferred_element_type=F32)
            p = _sink_softmax(s, sinks_ref[hs, :])
            outs.append(jnp.dot(p.astype(BF16), vb[:, cs], preferred_element_type=F32))
        o_ref[i] = jnp.concatenate(outs, axis=0)
        return carry

    lax.fori_loop(0, nb, body, 0)


def _attn_decode(q, k_win, v_win, sinks_col, *, bb=16):
    db = q.shape[0]
    w = k_win.shape[1]
    return pl.pallas_call(
        _attn_decode_kernel,
        out_shape=jax.ShapeDtypeStruct((db, N_HEADS, HEAD_DIM), F32),
        grid=(db // bb,),
        in_specs=[
            pl.BlockSpec((N_HEADS, 1), lambda i: (0, 0)),
            pl.BlockSpec((bb, N_HEADS, HEAD_DIM), lambda i: (i, 0, 0)),
            pl.BlockSpec((bb, w, KV_WIDTH), lambda i: (i, 0, 0)),
            pl.BlockSpec((bb, w, KV_WIDTH), lambda i: (i, 0, 0)),
        ],
        out_specs=pl.BlockSpec((bb, N_HEADS, HEAD_DIM), lambda i: (i, 0, 0)),
        compiler_params=_params(("parallel",), 32),
        name="attn_decode",
    )(sinks_col, q, k_win, v_win)


def _pool_prompt_kernel(*refs, ts):
    ng = len(POOL_WINDOWS)
    u_refs, halo_refs = refs[:ng], refs[ng:2 * ng]
    wp_ref, sc_ref, o_ref, ext_ref = refs[2 * ng:]
    t = pl.program_id(1)
    halo = 16
    pos = t * ts + lax.broadcasted_iota(jnp.int32, (ts, 1), 0)
    gw = wp_ref.shape[-1]
    for g, w in enumerate(POOL_WINDOWS):
        x = u_refs[g][...]
        ext_ref[0:halo, :] = jnp.where(t > 0, halo_refs[g][...], 0.0)
        ext_ref[halo:halo + ts, :] = x
        acc = x
        for k in range(1, w):
            acc = acc + ext_ref[halo - k:halo - k + ts, :]
        cnt = jnp.minimum(pos + 1, w).astype(F32)
        d = acc / cnt - x
        y = jnp.dot(d.astype(BF16), wp_ref[g], preferred_element_type=F32)
        o_ref[:, g * gw:(g + 1) * gw] = (y * sc_ref[:, g * gw:(g + 1) * gw]).astype(o_ref.dtype)


def _pool_prompt(p3, w_pool, scale, *, ts=512):
    b, s, _ = p3.shape
    ng, gw, _ = w_pool.shape
    u_blk0 = (ATTN_WIDTH + 2 * KV_WIDTH) // gw
    halo = 16
    u_specs = [pl.BlockSpec((None, ts, gw), functools.partial(lambda i, t, g: (i, t, u_blk0 + g), g=g))
               for g in range(ng)]
    halo_specs = [
        pl.BlockSpec((None, halo, gw),
                     functools.partial(lambda i, t, g: (i, jnp.maximum(t * (ts // halo) - 1, 0), u_blk0 + g), g=g))
        for g in range(ng)]
    return pl.pallas_call(
        functools.partial(_pool_prompt_kernel, ts=ts),
        out_shape=jax.ShapeDtypeStruct((b, s, ng * gw), BF16),
        grid=(b, s // ts),
        in_specs=u_specs + halo_specs + [
            pl.BlockSpec((ng, gw, gw), lambda i, t: (0, 0, 0)),
            pl.BlockSpec((1, ng * gw), lambda i, t: (0, 0)),
        ],
        out_specs=pl.BlockSpec((None, ts, ng * gw), lambda i, t: (i, t, 0)),
        scratch_shapes=[pltpu.VMEM((halo + ts, gw), F32)],
        compiler_params=_params(("parallel", "parallel"), 32),
        name="pool_prompt",
    )(*([p3] * (2 * ng)), w_pool, scale)


def _pool_decode_kernel(st_ref, u_ref, wp_ref, sc_ref, o_ref):
    gw = wp_ref.shape[-1]
    rows = st_ref.shape[1]
    for g, w in enumerate(POOL_WINDOWS):
        cs = slice(g * gw, (g + 1) * gw)
        x = u_ref[:, cs]
        acc = x
        for k in range(1, w):
            acc = acc + st_ref[:, rows - k, cs]
        d = acc / float(min(PAST_LEN + 1, w)) - x
        y = jnp.dot(d.astype(BF16), wp_ref[g], preferred_element_type=F32)
        o_ref[:, cs] = (y * sc_ref[:, cs]).astype(o_ref.dtype)


def _pool_decode(state, u_new, w_pool, scale, *, bb=32):
    db, rows, pw = state.shape
    ng, gw, _ = w_pool.shape
    return pl.pallas_call(
        _pool_decode_kernel,
        out_shape=jax.ShapeDtypeStruct((db, pw), BF16),
        grid=(db // bb,),
        in_specs=[
            pl.BlockSpec((bb, rows, pw), lambda i: (i, 0, 0)),
            pl.BlockSpec((bb, pw), lambda i: (i, 0)),
            pl.BlockSpec((ng, gw, gw), lambda i: (0, 0, 0)),
            pl.BlockSpec((1, pw), lambda i: (0, 0)),
        ],
        out_specs=pl.BlockSpec((bb, pw), lambda i: (i, 0)),
        compiler_params=_params(("parallel",), 32),
        name="pool_decode",
    )(state, u_new, w_pool, scale)


def _mix_kernel(x_ref, a_ref, pl_ref, wa_ref, wp_ref, o_ref):
    acc = jnp.dot(a_ref[...], wa_ref[...], preferred_element_type=F32)
    acc = acc + jnp.dot(pl_ref[...], wp_ref[...], preferred_element_type=F32)
    o_ref[...] = x_ref[...] + acc


def _mix(x, a, pooled, w_o, *, tm, tn):
    m, d = x.shape
    aw = a.shape[1]
    pw = pooled.shape[1]
    assert aw == pw
    return pl.pallas_call(
        _mix_kernel,
        out_shape=jax.ShapeDtypeStruct((m, d), F32),
        grid=(m // tm, d // tn),
        in_specs=[
            pl.BlockSpec((tm, tn), lambda i, j: (i, j)),
            pl.BlockSpec((tm, aw), lambda i, j: (i, 0)),
            pl.BlockSpec((tm, pw), lambda i, j: (i, 0)),
            pl.BlockSpec((aw, tn), lambda i, j: (0, j)),
            pl.BlockSpec((pw, tn), lambda i, j: (1, j)),
        ],
        out_specs=pl.BlockSpec((tm, tn), lambda i, j: (i, j)),
        compiler_params=_params(("parallel", "arbitrary"), 48),
        name="mix",
    )(x, a, pooled, w_o, w_o)


def _ffn_kernel(x_ref, g_ref, wg_ref, wu_ref, wd_ref, gf_ref, o_ref, h_ref, *, final_norm):
    f = pl.program_id(1)

    @pl.when(f == 0)
    def _():
        xv = x_ref[...]
        h_ref[...] = _rmsnorm(xv, g_ref[...]).astype(BF16)
        o_ref[...] = xv

    h = h_ref[...]
    gate = jnp.dot(h, wg_ref[...], preferred_element_type=F32)
    up = jnp.dot(h, wu_ref[...], preferred_element_type=F32)
    act = (gate * (1.0 / (1.0 + jnp.exp(-gate))) * up).astype(BF16)
    o_ref[...] += jnp.dot(act, wd_ref[...], preferred_element_type=F32)

    if final_norm:
        @pl.when(f == pl.num_programs(1) - 1)
        def _():
            o_ref[...] = _rmsnorm(o_ref[...], gf_ref[...])


def _ffn(x, g, wg, wu, wd, g_final, *, tm, tf, final_norm):
    m, d = x.shape
    dff = wg.shape[1]
    return pl.pallas_call(
        functools.partial(_ffn_kernel, final_norm=final_norm),
        out_shape=jax.ShapeDtypeStruct((m, d), F32),
        grid=(m // tm, dff // tf),
        in_specs=[
            pl.BlockSpec((tm, d), lambda i, f: (i, 0)),
            pl.BlockSpec((1, d), lambda i, f: (0, 0)),
            pl.BlockSpec((d, tf), lambda i, f: (0, f)),
            pl.BlockSpec((d, tf), lambda i, f: (0, f)),
            pl.BlockSpec((tf, d), lambda i, f: (f, 0)),
            pl.BlockSpec((1, d), lambda i, f: (0, 0)),
        ],
        out_specs=pl.BlockSpec((tm, d), lambda i, f: (i, 0)),
        scratch_shapes=[pltpu.VMEM((tm, d), BF16)],
        compiler_params=_params(("parallel", "arbitrary"), 60),
        name="ffn",
    )(x, g, wg, wu, wd, g_final)


def _decode_mask_is_trailing_window(w, t):
    pos_q = PAST_LEN + np.arange(t)
    pos_k = PAST_LEN - w + np.arange(w + t)
    mask = (pos_k[None, :] <= pos_q[:, None]) & (pos_q[:, None] - pos_k[None, :] < WINDOW)
    want = np.zeros_like(mask)
    want[:, t:] = True
    return bool((mask == want).all())


def kernel(x_prompt, x_sample, cache_k_win, cache_v_win, state_pool, g_mix, w_in, sinks, w_pool, pool_scale,
           w_o, g_ffn, w_gate, w_up, w_down, g_final):
    bsz, seq, d = x_prompt.shape
    db, t_new, _ = x_sample.shape
    depth = w_in.shape[0]
    w_rows = cache_k_win.shape[2]
    assert t_new == 1 and _decode_mask_is_trailing_window(w_rows, t_new)
    pw = min(WINDOW, seq)
    u0 = ATTN_WIDTH + 2 * KV_WIDTH

    xp = x_prompt.reshape(bsz * seq, d)
    xs = x_sample.reshape(db * t_new, d)
    gfin = g_final.reshape(1, d)
    outs = [[] for _ in range(6)]
    for l in range(depth):
        last = l == depth - 1
        w_in_l = w_in[l].astype(BF16)
        w_o_l = w_o[l].astype(BF16)
        w_pool_l = w_pool[l].astype(BF16)
        wg_l, wu_l, wd_l = w_gate[l].astype(BF16), w_up[l].astype(BF16), w_down[l].astype(BF16)
        gm, gf = g_mix[l].reshape(1, d), g_ffn[l].reshape(1, d)
        scale = pool_scale[l].reshape(1, -1)

        p = _proj(xp, gm, w_in_l, tm=512, tn=512)
        p3 = p.reshape(bsz, seq, -1)
        a = _attn_prompt(p3, sinks[l])
        pooled = _pool_prompt(p3, w_pool_l, scale)
        x1 = _mix(xp, a.reshape(bsz * seq, -1), pooled.reshape(bsz * seq, -1), w_o_l, tm=1024, tn=512)
        xp = _ffn(x1, gf, wg_l, wu_l, wd_l, gfin, tm=512, tf=256, final_norm=last)
        outs[0].append(p3[:, seq - pw:, ATTN_WIDTH:ATTN_WIDTH + KV_WIDTH].reshape(bsz, pw, N_KV_HEADS, HEAD_DIM))
        outs[1].append(p3[:, seq - pw:, ATTN_WIDTH + KV_WIDTH:u0].reshape(bsz, pw, N_KV_HEADS, HEAD_DIM))
        outs[2].append(p3[:, seq - POOL_STATE_ROWS:, u0:])

        ps = _proj(xs, gm, w_in_l, tm=db, tn=512)
        k_new = ps[:, ATTN_WIDTH:ATTN_WIDTH + KV_WIDTH].reshape(db, 1, N_KV_HEADS, HEAD_DIM)
        v_new = ps[:, ATTN_WIDTH + KV_WIDTH:u0].reshape(db, 1, N_KV_HEADS, HEAD_DIM)
        u_new = ps[:, u0:]
        k_win = jnp.concatenate([cache_k_win[l], k_new], axis=1)[:, -w_rows:]
        v_win = jnp.concatenate([cache_v_win[l], v_new], axis=1)[:, -w_rows:]
        a_s = _attn_decode(ps[:, :ATTN_WIDTH].reshape(db, N_HEADS, HEAD_DIM),
                           k_win.reshape(db, w_rows, KV_WIDTH), v_win.reshape(db, w_rows, KV_WIDTH),
                           sinks[l].reshape(N_HEADS, 1))
        pooled_s = _pool_decode(state_pool[l], u_new, w_pool_l, scale)
        x1s = _mix(xs, a_s.reshape(db, ATTN_WIDTH).astype(BF16), pooled_s, w_o_l, tm=db, tn=512)
        xs = _ffn(x1s, gf, wg_l, wu_l, wd_l, gfin, tm=db, tf=256, final_norm=last)
        outs[3].append(k_win)
        outs[4].append(v_win)
        outs[5].append(jnp.concatenate([state_pool[l], u_new[:, None, :]], axis=1)[:, -POOL_STATE_ROWS:])

    return (xp.reshape(bsz, seq, d), xs.reshape(db, t_new, d)) + tuple(jnp.stack(o) for o in outs)
```

```python
import functools

import jax
import jax.numpy as jnp
import numpy as np
from jax import lax
from jax.experimental import pallas as pl
from jax.experimental.pallas import tpu as pltpu

HEAD_DIM = 64
N_KV_HEADS = 4
GROUP = 8
N_HEADS = N_KV_HEADS * GROUP
ATTN_WIDTH = N_HEADS * HEAD_DIM
KV_WIDTH = N_KV_HEADS * HEAD_DIM
WINDOW = 128
BLOCK = 128
PAST_LEN = 8192
POOL_WINDOWS = (2, 4, 8, 16)
POOL_STATE_ROWS = max(POOL_WINDOWS) - 1
NORM_EPS = 1e-5
NEG_INF = -1e30

BF16 = jnp.bfloat16
F32 = jnp.float32
MIB = 1024 * 1024


def _rmsnorm(xf, g):
    return xf * lax.rsqrt(jnp.mean(xf * xf, axis=-1, keepdims=True) + NORM_EPS) * g


def _params(semantics, vmem_mib):
    return pltpu.CompilerParams(dimension_semantics=semantics, vmem_limit_bytes=vmem_mib * MIB)


def _rmsnorm_rows(src_ref, g_ref, dst_ref, chunk=256):
    rows = src_ref.shape[0]
    chunk = min(chunk, rows)
    for r0 in range(0, rows, chunk):
        dst_ref[r0:r0 + chunk, :] = _rmsnorm(src_ref[r0:r0 + chunk, :], g_ref[...]).astype(dst_ref.dtype)


def _proj_kernel(x_ref, g_ref, w_ref, p_ref, h_ref):
    @pl.when(pl.program_id(1) == 0)
    def _():
        _rmsnorm_rows(x_ref, g_ref, h_ref)

    p_ref[...] = jnp.dot(h_ref[...], w_ref[...], preferred_element_type=F32)


def _proj(x, g, w, *, tm, tn, vmem_mib=48):
    m, d = x.shape
    n = w.shape[1]
    return pl.pallas_call(
        _proj_kernel,
        out_shape=jax.ShapeDtypeStruct((m, n), F32),
        grid=(m // tm, n // tn),
        in_specs=[
            pl.BlockSpec((tm, d), lambda i, j: (i, 0)),
            pl.BlockSpec((1, d), lambda i, j: (0, 0)),
            pl.BlockSpec((d, tn), lambda i, j: (0, j)),
        ],
        out_specs=pl.BlockSpec((tm, tn), lambda i, j: (i, j)),
        scratch_shapes=[pltpu.VMEM((tm, d), BF16)],
        compiler_params=_params(("parallel", "arbitrary"), vmem_mib),
        name="proj",
    )(x, g, w)


def _sink_softmax(s, sink):
    m = jnp.maximum(jnp.max(s, axis=-1, keepdims=True), sink)
    e = jnp.exp(s - m)
    den = jnp.sum(e, axis=-1, keepdims=True) + jnp.exp(sink - m)
    return e * (1.0 / den)


def _cast_blocks(rows, steps):
    for nblk in range(min(steps, rows // 16), 0, -1):
        if rows % nblk == 0 and (rows // nblk) % 16 == 0:
            return nblk
    raise ValueError(f"no bf16-tileable row split of {rows} rows")


def _cast_job_specs(w, steps, step_of):
    rows, cols = w.shape
    nblk = _cast_blocks(rows, steps)
    spec = pl.BlockSpec((rows // nblk, cols), lambda *idx: (jnp.minimum(step_of(*idx), nblk - 1), 0))
    return nblk, spec, jax.ShapeDtypeStruct(w.shape, BF16)


def _run_cast_jobs(step, blocks, srcs, dsts):
    for nblk, src, dst in zip(blocks, srcs, dsts):
        @pl.when(step < nblk)
        def _():
            dst[...] = src[...].astype(dst.dtype)


def _attn_prompt_kernel(*refs, cast_blocks):
    nc = len(cast_blocks)
    sinks_ref, q_ref, kvp_ref, kvc_ref = refs[:4]
    o_ref = refs[4 + nc]
    _run_cast_jobs(pl.program_id(0) * pl.num_programs(1) + pl.program_id(1), cast_blocks,
                   refs[4:4 + nc], refs[5 + nc:])
    nblk = pl.program_id(1)
    qi = lax.broadcasted_iota(jnp.int32, (BLOCK, 2 * BLOCK), 0)
    kj = lax.broadcasted_iota(jnp.int32, (BLOCK, 2 * BLOCK), 1) - BLOCK
    mask = (kj <= qi) & (qi - kj < WINDOW) & (nblk * BLOCK + kj >= 0)
    scale = HEAD_DIM ** -0.5
    for kh in range(N_KV_HEADS):
        ks = slice(kh * HEAD_DIM, (kh + 1) * HEAD_DIM)
        vs = slice(KV_WIDTH + kh * HEAD_DIM, KV_WIDTH + (kh + 1) * HEAD_DIM)
        kcat = jnp.concatenate([kvp_ref[:, ks], kvc_ref[:, ks]], axis=0).astype(BF16)
        vcat = jnp.concatenate([kvp_ref[:, vs], kvc_ref[:, vs]], axis=0).astype(BF16)
        for gp in range(GROUP // 2):
            pair = []
            for gi in range(2):
                h = kh * GROUP + gp * 2 + gi
                qh = (q_ref[:, h * HEAD_DIM:(h + 1) * HEAD_DIM] * scale).astype(BF16)
                s = lax.dot_general(qh, kcat, (((1,), (1,)), ((), ())), preferred_element_type=F32)
                s = jnp.where(mask, s, NEG_INF)
                p = _sink_softmax(s, sinks_ref[h])
                pair.append(jnp.dot(p.astype(BF16), vcat, preferred_element_type=F32))
            c0 = (kh * GROUP + gp * 2) * HEAD_DIM
            o_ref[:, c0:c0 + 2 * HEAD_DIM] = jnp.concatenate(pair, axis=1).astype(o_ref.dtype)


def _attn_prompt(p3, sinks, cast_ws=()):
    b, s, _ = p3.shape
    nb = s // BLOCK
    kv_blk = ATTN_WIDTH // (2 * KV_WIDTH)
    jobs = [_cast_job_specs(w, b * nb, lambda i, n: i * nb + n) for w in cast_ws]
    return pl.pallas_call(
        functools.partial(_attn_prompt_kernel, cast_blocks=tuple(j[0] for j in jobs)),
        out_shape=[jax.ShapeDtypeStruct((b, s, ATTN_WIDTH), BF16)] + [j[2] for j in jobs],
        grid=(b, nb),
        in_specs=[
            pl.BlockSpec(memory_space=pltpu.SMEM),
            pl.BlockSpec((None, BLOCK, ATTN_WIDTH), lambda i, n: (i, n, 0)),
            pl.BlockSpec((None, BLOCK, 2 * KV_WIDTH), lambda i, n: (i, jnp.maximum(n - 1, 0), kv_blk)),
            pl.BlockSpec((None, BLOCK, 2 * KV_WIDTH), lambda i, n: (i, n, kv_blk)),
        ] + [j[1] for j in jobs],
        out_specs=[pl.BlockSpec((None, BLOCK, ATTN_WIDTH), lambda i, n: (i, n, 0))] + [j[1] for j in jobs],
        compiler_params=_params(("arbitrary", "arbitrary"), 48),
        name="attn_prompt",
    )(sinks, p3, p3, p3, *cast_ws)


def _attn_decode_kernel(sinks_ref, q_ref, kt_ref, vt_ref, o_ref):
    scale = HEAD_DIM ** -0.5
    row_kv = lax.broadcasted_iota(jnp.int32, (N_HEADS, KV_WIDTH), 0) // GROUP
    col_kv = lax.broadcasted_iota(jnp.int32, (N_HEADS, KV_WIDTH), 1) // HEAD_DIM
    q = q_ref[...] * scale
    q_bd = jnp.where(row_kv == col_kv, jnp.concatenate([q] * N_KV_HEADS, axis=-1), 0.0).astype(BF16)
    s = lax.dot_general(q_bd, kt_ref[...].astype(BF16), (((2,), (1,)), ((0,), (0,))),
                        preferred_element_type=F32)
    p = _sink_softmax(s, sinks_ref[...])
    o_full = lax.dot_general(p.astype(BF16), vt_ref[...].astype(BF16), (((2,), (2,)), ((0,), (0,))),
                             preferred_element_type=F32)
    head_kv = lax.broadcasted_iota(jnp.int32, (N_HEADS, HEAD_DIM), 0) // GROUP
    o = jnp.zeros(o_ref.shape, F32)
    for kh in range(N_KV_HEADS):
        o = o + jnp.where(head_kv == kh, o_full[:, :, kh * HEAD_DIM:(kh + 1) * HEAD_DIM], 0.0)
    o_ref[...] = o


def _attn_decode(q, kt_win, vt_win, sinks_col, *, bb=16):
    db = q.shape[0]
    w = kt_win.shape[2]
    return pl.pallas_call(
        _attn_decode_kernel,
        out_shape=jax.ShapeDtypeStruct((db, N_HEADS, HEAD_DIM), F32),
        grid=(db // bb,),
        in_specs=[
            pl.BlockSpec((N_HEADS, 1), lambda i: (0, 0)),
            pl.BlockSpec((bb, N_HEADS, HEAD_DIM), lambda i: (i, 0, 0)),
            pl.BlockSpec((bb, KV_WIDTH, w), lambda i: (i, 0, 0)),
            pl.BlockSpec((bb, KV_WIDTH, w), lambda i: (i, 0, 0)),
        ],
        out_specs=pl.BlockSpec((bb, N_HEADS, HEAD_DIM), lambda i: (i, 0, 0)),
        compiler_params=_params(("parallel",), 32),
        name="attn_decode",
    )(sinks_col, q, kt_win, vt_win)


def _pool_prompt_kernel(*refs, ts, cast_blocks):
    ng = len(POOL_WINDOWS)
    nc = len(cast_blocks)
    u_refs, halo_refs = refs[:ng], refs[ng:2 * ng]
    wp_ref, sc_ref = refs[2 * ng:2 * ng + 2]
    o_ref = refs[2 * ng + 2 + nc]
    ext_ref = refs[-1]
    _run_cast_jobs(pl.program_id(0) * pl.num_programs(1) + pl.program_id(1), cast_blocks,
                   refs[2 * ng + 2:2 * ng + 2 + nc], refs[2 * ng + 3 + nc:-1])
    t = pl.program_id(1)
    halo = 16
    pos = t * ts + lax.broadcasted_iota(jnp.int32, (ts, 1), 0)
    gw = wp_ref.shape[-1]
    for g, w in enumerate(POOL_WINDOWS):
        x = u_refs[g][...]
        ext_ref[0:halo, :] = jnp.where(t > 0, halo_refs[g][...], 0.0)
        ext_ref[halo:halo + ts, :] = x
        acc = x
        for k in range(1, w):
            acc = acc + ext_ref[halo - k:halo - k + ts, :]
        cnt = jnp.minimum(pos + 1, w).astype(F32)
        d = acc / cnt - x
        y = jnp.dot(d.astype(BF16), wp_ref[g].astype(BF16), preferred_element_type=F32)
        o_ref[:, g * gw:(g + 1) * gw] = (y * sc_ref[:, g * gw:(g + 1) * gw]).astype(o_ref.dtype)


def _pool_prompt(p3, w_pool, scale, cast_ws=(), *, ts=512):
    b, s, _ = p3.shape
    nt = s // ts
    jobs = [_cast_job_specs(w, b * nt, lambda i, t: i * nt + t) for w in cast_ws]
    ng, gw, _ = w_pool.shape
    u_blk0 = (ATTN_WIDTH + 2 * KV_WIDTH) // gw
    halo = 16
    u_specs = [pl.BlockSpec((None, ts, gw), functools.partial(lambda i, t, g: (i, t, u_blk0 + g), g=g))
               for g in range(ng)]
    halo_specs = [
        pl.BlockSpec((None, halo, gw),
                     functools.partial(lambda i, t, g: (i, jnp.maximum(t * (ts // halo) - 1, 0), u_blk0 + g), g=g))
        for g in range(ng)]
    return pl.pallas_call(
        functools.partial(_pool_prompt_kernel, ts=ts, cast_blocks=tuple(j[0] for j in jobs)),
        out_shape=[jax.ShapeDtypeStruct((b, s, ng * gw), BF16)] + [j[2] for j in jobs],
        grid=(b, nt),
        in_specs=u_specs + halo_specs + [
            pl.BlockSpec((ng, gw, gw), lambda i, t: (0, 0, 0)),
            pl.BlockSpec((1, ng * gw), lambda i, t: (0, 0)),
        ] + [j[1] for j in jobs],
        out_specs=[pl.BlockSpec((None, ts, ng * gw), lambda i, t: (i, t, 0))] + [j[1] for j in jobs],
        scratch_shapes=[pltpu.VMEM((halo + ts, gw), F32)],
        compiler_params=_params(("arbitrary", "arbitrary"), 48),
        name="pool_prompt",
    )(*([p3] * (2 * ng)), w_pool, scale, *cast_ws)


def _pool_decode_kernel(st_ref, u_ref, wp_ref, sc_ref, o_ref):
    gw = wp_ref.shape[-1]
    rows = st_ref.shape[0]
    for g, w in enumerate(POOL_WINDOWS):
        cs = slice(g * gw, (g + 1) * gw)
        x = u_ref[:, cs]
        acc = x
        for k in range(1, w):
            acc = acc + st_ref[rows - k, :, cs]
        d = acc / float(min(PAST_LEN + 1, w)) - x
        y = jnp.dot(d.astype(BF16), wp_ref[g].astype(BF16), preferred_element_type=F32)
        o_ref[:, cs] = (y * sc_ref[:, cs]).astype(o_ref.dtype)


def _pool_decode(state_t, u_new, w_pool, scale, *, bb=32):
    rows, db, pw = state_t.shape
    ng, gw, _ = w_pool.shape
    return pl.pallas_call(
        _pool_decode_kernel,
        out_shape=jax.ShapeDtypeStruct((db, pw), BF16),
        grid=(db // bb,),
        in_specs=[
            pl.BlockSpec((rows, bb, pw), lambda i: (0, i, 0)),
            pl.BlockSpec((bb, pw), lambda i: (i, 0)),
            pl.BlockSpec((ng, gw, gw), lambda i: (0, 0, 0)),
            pl.BlockSpec((1, pw), lambda i: (0, 0)),
        ],
        out_specs=pl.BlockSpec((bb, pw), lambda i: (i, 0)),
        compiler_params=_params(("parallel",), 32),
        name="pool_decode",
    )(state_t, u_new, w_pool, scale)


def _mix_kernel(x_ref, a_ref, pl_ref, wa_ref, wp_ref, o_ref):
    acc = jnp.dot(a_ref[...], wa_ref[...], preferred_element_type=F32)
    acc = acc + jnp.dot(pl_ref[...], wp_ref[...], preferred_element_type=F32)
    o_ref[...] = x_ref[...] + acc


def _mix(x, a, pooled, w_o, *, tm, tn):
    m, d = x.shape
    aw = a.shape[1]
    pw = pooled.shape[1]
    assert aw == pw
    return pl.pallas_call(
        _mix_kernel,
        out_shape=jax.ShapeDtypeStruct((m, d), F32),
        grid=(m // tm, d // tn),
        in_specs=[
            pl.BlockSpec((tm, tn), lambda i, j: (i, j)),
            pl.BlockSpec((tm, aw), lambda i, j: (i, 0)),
            pl.BlockSpec((tm, pw), lambda i, j: (i, 0)),
            pl.BlockSpec((aw, tn), lambda i, j: (0, j)),
            pl.BlockSpec((pw, tn), lambda i, j: (1, j)),
        ],
        out_specs=pl.BlockSpec((tm, tn), lambda i, j: (i, j)),
        compiler_params=_params(("parallel", "arbitrary"), 48),
        name="mix",
    )(x, a, pooled, w_o, w_o)


def _ffn_kernel(x_ref, g_ref, wg_ref, wu_ref, wd_ref, gf_ref, o_ref, h_ref, *, final_norm):
    f = pl.program_id(1)

    @pl.when(f == 0)
    def _():
        _rmsnorm_rows(x_ref, g_ref, h_ref)
        o_ref[...] = x_ref[...]

    h = h_ref[...]
    gate = jnp.dot(h, wg_ref[...], preferred_element_type=F32)
    up = jnp.dot(h, wu_ref[...], preferred_element_type=F32)
    act = (gate * (1.0 / (1.0 + jnp.exp(-gate))) * up).astype(BF16)
    o_ref[...] += jnp.dot(act, wd_ref[...], preferred_element_type=F32)

    if final_norm:
        @pl.when(f == pl.num_programs(1) - 1)
        def _():
            o_ref[...] = _rmsnorm(o_ref[...], gf_ref[...])


def _ffn(x, g, wg, wu, wd, g_final, *, tm, tf, final_norm):
    m, d = x.shape
    dff = wg.shape[1]
    return pl.pallas_call(
        functools.partial(_ffn_kernel, final_norm=final_norm),
        out_shape=jax.ShapeDtypeStruct((m, d), F32),
        grid=(m // tm, dff // tf),
        in_specs=[
            pl.BlockSpec((tm, d), lambda i, f: (i, 0)),
            pl.BlockSpec((1, d), lambda i, f: (0, 0)),
            pl.BlockSpec((d, tf), lambda i, f: (0, f)),
            pl.BlockSpec((d, tf), lambda i, f: (0, f)),
            pl.BlockSpec((tf, d), lambda i, f: (f, 0)),
            pl.BlockSpec((1, d), lambda i, f: (0, 0)),
        ],
        out_specs=pl.BlockSpec((tm, d), lambda i, f: (i, 0)),
        scratch_shapes=[pltpu.VMEM((tm, d), BF16)],
        compiler_params=_params(("parallel", "arbitrary"), 60),
        name="ffn",
    )(x, g, wg, wu, wd, g_final)


def _decode_mask_is_trailing_window(w, t):
    pos_q = PAST_LEN + np.arange(t)
    pos_k = PAST_LEN - w + np.arange(w + t)
    mask = (pos_k[None, :] <= pos_q[:, None]) & (pos_q[:, None] - pos_k[None, :] < WINDOW)
    want = np.zeros_like(mask)
    want[:, t:] = True
    return bool((mask == want).all())


def kernel(x_prompt, x_sample, cache_k_win, cache_v_win, state_pool, g_mix, w_in, sinks, w_pool, pool_scale,
           w_o, g_ffn, w_gate, w_up, w_down, g_final):
    bsz, seq, d = x_prompt.shape
    db, t_new, _ = x_sample.shape
    depth = w_in.shape[0]
    w_rows = cache_k_win.shape[2]
    assert t_new == 1 and _decode_mask_is_trailing_window(w_rows, t_new)
    pw = min(WINDOW, seq)
    u0 = ATTN_WIDTH + 2 * KV_WIDTH

    xp = x_prompt.reshape(bsz * seq, d)
    xs = x_sample.reshape(db * t_new, d)
    gfin = g_final.reshape(1, d)
    outs = [[] for _ in range(6)]
    for l in range(depth):
        last = l == depth - 1
        w_in_l = w_in[l].astype(BF16)
        w_pool_l = w_pool[l]
        gm, gf = g_mix[l].reshape(1, d), g_ffn[l].reshape(1, d)
        scale = pool_scale[l].reshape(1, -1)

        p = _proj(xp, gm, w_in_l, tm=1024, tn=512, vmem_mib=60)
        p3 = p.reshape(bsz, seq, -1)
        a, wg_l, wu_l, wd_l = _attn_prompt(p3, sinks[l], (w_gate[l], w_up[l], w_down[l]))
        pooled, w_o_l = _pool_prompt(p3, w_pool_l, scale, (w_o[l],))
        x1 = _mix(xp, a.reshape(bsz * seq, -1), pooled.reshape(bsz * seq, -1), w_o_l, tm=1024, tn=512)
        xp = _ffn(x1, gf, wg_l, wu_l, wd_l, gfin, tm=512, tf=256, final_norm=last)
        outs[0].append(p3[:, seq - pw:, ATTN_WIDTH:ATTN_WIDTH + KV_WIDTH].reshape(bsz, pw, N_KV_HEADS, HEAD_DIM))
        outs[1].append(p3[:, seq - pw:, ATTN_WIDTH + KV_WIDTH:u0].reshape(bsz, pw, N_KV_HEADS, HEAD_DIM))
        outs[2].append(p3[:, seq - POOL_STATE_ROWS:, u0:])

        ps = _proj(xs, gm, w_in_l, tm=db, tn=512)
        k_new = ps[:, ATTN_WIDTH:ATTN_WIDTH + KV_WIDTH].reshape(db, 1, N_KV_HEADS, HEAD_DIM)
        v_new = ps[:, ATTN_WIDTH + KV_WIDTH:u0].reshape(db, 1, N_KV_HEADS, HEAD_DIM)
        u_new = ps[:, u0:]
        kt_win = jnp.concatenate([jnp.transpose(cache_k_win[l], (0, 2, 3, 1)), jnp.transpose(k_new, (0, 2, 3, 1))],
                                 axis=3)[..., -w_rows:]
        vt_win = jnp.concatenate([jnp.transpose(cache_v_win[l], (0, 2, 3, 1)), jnp.transpose(v_new, (0, 2, 3, 1))],
                                 axis=3)[..., -w_rows:]
        a_s = _attn_decode(ps[:, :ATTN_WIDTH].reshape(db, N_HEADS, HEAD_DIM),
                           kt_win.reshape(db, KV_WIDTH, w_rows), vt_win.reshape(db, KV_WIDTH, w_rows),
                           sinks[l].reshape(N_HEADS, 1))
        state_t = jnp.transpose(state_pool[l], (1, 0, 2))
        pooled_s = _pool_decode(state_t, u_new, w_pool_l, scale)
        x1s = _mix(xs, a_s.reshape(db, ATTN_WIDTH).astype(BF16), pooled_s, w_o_l, tm=db, tn=512)
        xs = _ffn(x1s, gf, wg_l, wu_l, wd_l, gfin, tm=db, tf=256, final_norm=last)
        outs[3].append(jnp.transpose(kt_win, (0, 3, 1, 2)))
        outs[4].append(jnp.transpose(vt_win, (0, 3, 1, 2)))
        outs[5].append(jnp.transpose(jnp.concatenate([state_t, u_new[None]], axis=0)[-POOL_STATE_ROWS:], (1, 0, 2)))

    return (xp.reshape(bsz, seq, d), xs.reshape(db, t_new, d)) + tuple(jnp.stack(o) for o in outs)
```

```python
import functools

import jax
import jax.numpy as jnp
import numpy as np
from jax import lax
from jax.experimental import pallas as pl
from jax.experimental.pallas import tpu as pltpu

HEAD_DIM = 64
N_KV_HEADS = 4
GROUP = 8
N_HEADS = N_KV_HEADS * GROUP
ATTN_WIDTH = N_HEADS * HEAD_DIM
KV_WIDTH = N_KV_HEADS * HEAD_DIM
WINDOW = 128
BLOCK = 128
PAST_LEN = 8192
POOL_WINDOWS = (2, 4, 8, 16)
POOL_STATE_ROWS = max(POOL_WINDOWS) - 1
NORM_EPS = 1e-5
NEG_INF = -1e30

BF16 = jnp.bfloat16
F32 = jnp.float32
MIB = 1024 * 1024


def _rmsnorm(xf, g):
    return xf * lax.rsqrt(jnp.mean(xf * xf, axis=-1, keepdims=True) + NORM_EPS) * g


def _params(semantics, vmem_mib):
    return pltpu.CompilerParams(dimension_semantics=semantics, vmem_limit_bytes=vmem_mib * MIB)


def _rmsnorm_rows(src_ref, g_ref, dst_ref, chunk=256):
    rows = src_ref.shape[0]
    chunk = min(chunk, rows)
    for r0 in range(0, rows, chunk):
        dst_ref[r0:r0 + chunk, :] = _rmsnorm(src_ref[r0:r0 + chunk, :], g_ref[...]).astype(dst_ref.dtype)


def _proj_kernel(x_ref, g_ref, w_ref, p_ref, h_ref):
    @pl.when(pl.program_id(1) == 0)
    def _():
        _rmsnorm_rows(x_ref, g_ref, h_ref)

    p_ref[...] = jnp.dot(h_ref[...], w_ref[...], preferred_element_type=F32)


def _proj(x, g, w, *, tm, tn, vmem_mib=48):
    m, d = x.shape
    n = w.shape[1]
    return pl.pallas_call(
        _proj_kernel,
        out_shape=jax.ShapeDtypeStruct((m, n), F32),
        grid=(m // tm, n // tn),
        in_specs=[
            pl.BlockSpec((tm, d), lambda i, j: (i, 0)),
            pl.BlockSpec((1, d), lambda i, j: (0, 0)),
            pl.BlockSpec((d, tn), lambda i, j: (0, j)),
        ],
        out_specs=pl.BlockSpec((tm, tn), lambda i, j: (i, j)),
        scratch_shapes=[pltpu.VMEM((tm, d), BF16)],
        compiler_params=_params(("parallel", "arbitrary"), vmem_mib),
        name="proj",
    )(x, g, w)


def _sink_softmax(s, sink):
    m = jnp.maximum(jnp.max(s, axis=-1, keepdims=True), sink)
    e = jnp.exp(s - m)
    den = jnp.sum(e, axis=-1, keepdims=True) + jnp.exp(sink - m)
    return e * (1.0 / den)


def _cast_blocks(rows, steps):
    for nblk in range(min(steps, rows // 16), 0, -1):
        if rows % nblk == 0 and (rows // nblk) % 16 == 0:
            return nblk
    raise ValueError(f"no bf16-tileable row split of {rows} rows")


def _cast_job_specs(w, steps, step_of):
    rows, cols = w.shape
    nblk = _cast_blocks(rows, steps)
    spec = pl.BlockSpec((rows // nblk, cols), lambda *idx: (jnp.minimum(step_of(*idx), nblk - 1), 0))
    return nblk, spec, jax.ShapeDtypeStruct(w.shape, BF16)


def _run_cast_jobs(step, blocks, srcs, dsts):
    for nblk, src, dst in zip(blocks, srcs, dsts):
        @pl.when(step < nblk)
        def _():
            dst[...] = src[...].astype(dst.dtype)


def _attn_prompt_kernel(*refs, cast_blocks):
    nc = len(cast_blocks)
    sinks_ref, q_ref, kvp_ref, kvc_ref = refs[:4]
    o_ref = refs[4 + nc]
    _run_cast_jobs(pl.program_id(0) * pl.num_programs(1) + pl.program_id(1), cast_blocks,
                   refs[4:4 + nc], refs[5 + nc:])
    rows = GROUP * BLOCK
    row = lax.broadcasted_iota(jnp.int32, (rows, BLOCK), 0) % BLOCK
    col = lax.broadcasted_iota(jnp.int32, (rows, BLOCK), 1)
    from_prev = col > row
    scale = HEAD_DIM ** -0.5
    nt_dims = (((1,), (1,)), ((), ()))

    def per_head(fn, *cols):
        return jnp.concatenate(
            [fn(g, *(c[g * BLOCK:(g + 1) * BLOCK] for c in cols)) for g in range(GROUP)], axis=0)

    def attend(first_block):
        for kh in range(N_KV_HEADS):
            ks = slice(kh * HEAD_DIM, (kh + 1) * HEAD_DIM)
            vs = slice(KV_WIDTH + kh * HEAD_DIM, KV_WIDTH + (kh + 1) * HEAD_DIM)
            if first_block:
                kmat, vmat = kvc_ref[:, ks].astype(BF16), kvc_ref[:, vs].astype(BF16)
            else:
                kmat = jnp.concatenate([kvp_ref[:, ks], kvc_ref[:, ks]], axis=0).astype(BF16)
                vmat = jnp.concatenate([kvp_ref[:, vs], kvc_ref[:, vs]], axis=0).astype(BF16)
            h0 = kh * GROUP
            qg = jnp.concatenate(
                [(q_ref[:, (h0 + g) * HEAD_DIM:(h0 + g + 1) * HEAD_DIM] * scale).astype(BF16) for g in range(GROUP)],
                axis=0)
            s2 = lax.dot_general(qg, kmat, nt_dims, preferred_element_type=F32)
            if first_block:
                s = jnp.where(from_prev, NEG_INF, s2)
            else:
                s = jnp.where(from_prev, s2[:, :BLOCK], s2[:, BLOCK:])
            sink = jnp.concatenate([jnp.full((BLOCK, BLOCK), sinks_ref[h0 + g], F32) for g in range(GROUP)], axis=0)
            m = jnp.maximum(jnp.max(s, axis=-1, keepdims=True), sink)
            e = jnp.exp(s - m)
            den = jnp.sum(e, axis=-1, keepdims=True) + jnp.exp(sink - m)
            p = e * (1.0 / den)
            if not first_block:
                p = jnp.concatenate([jnp.where(from_prev, p, 0.0), jnp.where(from_prev, 0.0, p)], axis=1)
            og = jnp.dot(p.astype(BF16), vmat, preferred_element_type=F32)
            for gp in range(GROUP // 2):
                c0 = (h0 + 2 * gp) * HEAD_DIM
                o_ref[:, c0:c0 + 2 * HEAD_DIM] = jnp.concatenate(
                    [og[2 * gp * BLOCK:(2 * gp + 1) * BLOCK], og[(2 * gp + 1) * BLOCK:(2 * gp + 2) * BLOCK]],
                    axis=1).astype(o_ref.dtype)

    @pl.when(pl.program_id(1) == 0)
    def _():
        attend(True)

    @pl.when(pl.program_id(1) > 0)
    def _():
        attend(False)


def _attn_prompt(p3, sinks, cast_ws=()):
    assert WINDOW == BLOCK
    b, s, _ = p3.shape
    nb = s // BLOCK
    kv_blk = ATTN_WIDTH // (2 * KV_WIDTH)
    jobs = [_cast_job_specs(w, b * nb, lambda i, n: i * nb + n) for w in cast_ws]
    return pl.pallas_call(
        functools.partial(_attn_prompt_kernel, cast_blocks=tuple(j[0] for j in jobs)),
        out_shape=[jax.ShapeDtypeStruct((b, s, ATTN_WIDTH), BF16)] + [j[2] for j in jobs],
        grid=(b, nb),
        in_specs=[
            pl.BlockSpec(memory_space=pltpu.SMEM),
            pl.BlockSpec((None, BLOCK, ATTN_WIDTH), lambda i, n: (i, n, 0)),
            pl.BlockSpec((None, BLOCK, 2 * KV_WIDTH), lambda i, n: (i, jnp.maximum(n - 1, 0), kv_blk)),
            pl.BlockSpec((None, BLOCK, 2 * KV_WIDTH), lambda i, n: (i, n, kv_blk)),
        ] + [j[1] for j in jobs],
        out_specs=[pl.BlockSpec((None, BLOCK, ATTN_WIDTH), lambda i, n: (i, n, 0))] + [j[1] for j in jobs],
        compiler_params=_params(("arbitrary", "arbitrary"), 48),
        name="attn_prompt",
    )(sinks, p3, p3, p3, *cast_ws)


def _attn_decode_kernel(sinks_ref, q_ref, kt_ref, vt_ref, kn_ref, vn_ref, o_ref, ktw_ref, vtw_ref):
    bb, _, w = kt_ref.shape
    last = lax.broadcasted_iota(jnp.int32, (KV_WIDTH, w), 1) == w - 1
    for old_ref, new_ref, win_ref in ((kt_ref, kn_ref, ktw_ref), (vt_ref, vn_ref, vtw_ref)):
        for b in range(bb):
            shifted = pltpu.roll(old_ref[b], w - 1, axis=1)
            newest = jnp.broadcast_to(new_ref[:, b:b + 1], (KV_WIDTH, w))
            win_ref[b] = jnp.where(last, newest, shifted)

    scale = HEAD_DIM ** -0.5
    row_kv = lax.broadcasted_iota(jnp.int32, (N_HEADS, KV_WIDTH), 0) // GROUP
    col_kv = lax.broadcasted_iota(jnp.int32, (N_HEADS, KV_WIDTH), 1) // HEAD_DIM
    q = q_ref[...] * scale
    q_bd = jnp.where(row_kv == col_kv, jnp.concatenate([q] * N_KV_HEADS, axis=-1), 0.0).astype(BF16)
    s = lax.dot_general(q_bd, ktw_ref[...].astype(BF16), (((2,), (1,)), ((0,), (0,))),
                        preferred_element_type=F32)
    p = _sink_softmax(s, sinks_ref[...])
    o_full = lax.dot_general(p.astype(BF16), vtw_ref[...].astype(BF16), (((2,), (2,)), ((0,), (0,))),
                             preferred_element_type=F32)
    head_kv = lax.broadcasted_iota(jnp.int32, (N_HEADS, HEAD_DIM), 0) // GROUP
    o = jnp.zeros(o_ref.shape, F32)
    for kh in range(N_KV_HEADS):
        o = o + jnp.where(head_kv == kh, o_full[:, :, kh * HEAD_DIM:(kh + 1) * HEAD_DIM], 0.0)
    o_ref[...] = o


def _attn_decode(q, kt_old, vt_old, k_new, v_new, sinks_col, *, bb=16):
    db = q.shape[0]
    w = kt_old.shape[2]
    nblk = db // bb
    win_spec = pl.BlockSpec((bb, KV_WIDTH, w), lambda i: (i, 0, 0))
    new_spec = pl.BlockSpec((None, KV_WIDTH, bb), lambda i: (i, 0, 0))
    kn, vn = (jnp.transpose(x.reshape(nblk, bb, KV_WIDTH), (0, 2, 1)) for x in (k_new, v_new))
    return pl.pallas_call(
        _attn_decode_kernel,
        out_shape=[jax.ShapeDtypeStruct((db, N_HEADS, HEAD_DIM), F32),
                   jax.ShapeDtypeStruct(kt_old.shape, F32), jax.ShapeDtypeStruct(vt_old.shape, F32)],
        grid=(nblk,),
        in_specs=[
            pl.BlockSpec((N_HEADS, 1), lambda i: (0, 0)),
            pl.BlockSpec((bb, N_HEADS, HEAD_DIM), lambda i: (i, 0, 0)),
            win_spec, win_spec, new_spec, new_spec,
        ],
        out_specs=[pl.BlockSpec((bb, N_HEADS, HEAD_DIM), lambda i: (i, 0, 0)), win_spec, win_spec],
        compiler_params=_params(("parallel",), 32),
        name="attn_decode",
    )(sinks_col, q, kt_old, vt_old, kn, vn)


def _pool_prompt_kernel(*refs, ts, cast_blocks):
    ng = len(POOL_WINDOWS)
    nc = len(cast_blocks)
    u_refs, halo_refs = refs[:ng], refs[ng:2 * ng]
    wp_ref, sc_ref = refs[2 * ng:2 * ng + 2]
    o_ref = refs[2 * ng + 2 + nc]
    ext_ref = refs[-1]
    _run_cast_jobs(pl.program_id(0) * pl.num_programs(1) + pl.program_id(1), cast_blocks,
                   refs[2 * ng + 2:2 * ng + 2 + nc], refs[2 * ng + 3 + nc:-1])
    t = pl.program_id(1)
    halo = 16
    pos = t * ts + lax.broadcasted_iota(jnp.int32, (ts, 1), 0)
    gw = wp_ref.shape[-1]
    for g, w in enumerate(POOL_WINDOWS):
        x = u_refs[g][...]
        ext_ref[0:halo, :] = jnp.where(t > 0, halo_refs[g][...], 0.0)
        ext_ref[halo:halo + ts, :] = x
        acc = x
        for k in range(1, w):
            acc = acc + ext_ref[halo - k:halo - k + ts, :]
        cnt = jnp.minimum(pos + 1, w).astype(F32)
        d = acc / cnt - x
        y = jnp.dot(d.astype(BF16), wp_ref[g].astype(BF16), preferred_element_type=F32)
        o_ref[:, g * gw:(g + 1) * gw] = (y * sc_ref[:, g * gw:(g + 1) * gw]).astype(o_ref.dtype)


def _pool_prompt(p3, w_pool, scale, cast_ws=(), *, ts=512):
    b, s, _ = p3.shape
    nt = s // ts
    jobs = [_cast_job_specs(w, b * nt, lambda i, t: i * nt + t) for w in cast_ws]
    ng, gw, _ = w_pool.shape
    u_blk0 = (ATTN_WIDTH + 2 * KV_WIDTH) // gw
    halo = 16
    u_specs = [pl.BlockSpec((None, ts, gw), functools.partial(lambda i, t, g: (i, t, u_blk0 + g), g=g))
               for g in range(ng)]
    halo_specs = [
        pl.BlockSpec((None, halo, gw),
                     functools.partial(lambda i, t, g: (i, jnp.maximum(t * (ts // halo) - 1, 0), u_blk0 + g), g=g))
        for g in range(ng)]
    return pl.pallas_call(
        functools.partial(_pool_prompt_kernel, ts=ts, cast_blocks=tuple(j[0] for j in jobs)),
        out_shape=[jax.ShapeDtypeStruct((b, s, ng * gw), BF16)] + [j[2] for j in jobs],
        grid=(b, nt),
        in_specs=u_specs + halo_specs + [
            pl.BlockSpec((ng, gw, gw), lambda i, t: (0, 0, 0)),
            pl.BlockSpec((1, ng * gw), lambda i, t: (0, 0)),
        ] + [j[1] for j in jobs],
        out_specs=[pl.BlockSpec((None, ts, ng * gw), lambda i, t: (i, t, 0))] + [j[1] for j in jobs],
        scratch_shapes=[pltpu.VMEM((halo + ts, gw), F32)],
        compiler_params=_params(("arbitrary", "arbitrary"), 48),
        name="pool_prompt",
    )(*([p3] * (2 * ng)), w_pool, scale, *cast_ws)


def _pool_decode_kernel(st_ref, u_ref, wp_ref, sc_ref, o_ref, new_st_ref):
    gw = wp_ref.shape[-1]
    rows = st_ref.shape[0]
    for r in range(rows - 1):
        new_st_ref[r] = st_ref[r + 1]
    new_st_ref[rows - 1] = u_ref[...]
    for g, w in enumerate(POOL_WINDOWS):
        cs = slice(g * gw, (g + 1) * gw)
        x = u_ref[:, cs]
        acc = x
        for k in range(1, w):
            acc = acc + st_ref[rows - k, :, cs]
        d = acc / float(min(PAST_LEN + 1, w)) - x
        y = jnp.dot(d.astype(BF16), wp_ref[g].astype(BF16), preferred_element_type=F32)
        o_ref[:, cs] = (y * sc_ref[:, cs]).astype(o_ref.dtype)


def _pool_decode(state_t, u_new, w_pool, scale, *, bb=32):
    rows, db, pw = state_t.shape
    ng, gw, _ = w_pool.shape
    st_spec = pl.BlockSpec((rows, bb, pw), lambda i: (0, i, 0))
    return pl.pallas_call(
        _pool_decode_kernel,
        out_shape=[jax.ShapeDtypeStruct((db, pw), BF16), jax.ShapeDtypeStruct(state_t.shape, F32)],
        grid=(db // bb,),
        in_specs=[
            st_spec,
            pl.BlockSpec((bb, pw), lambda i: (i, 0)),
            pl.BlockSpec((ng, gw, gw), lambda i: (0, 0, 0)),
            pl.BlockSpec((1, pw), lambda i: (0, 0)),
        ],
        out_specs=[pl.BlockSpec((bb, pw), lambda i: (i, 0)), st_spec],
        compiler_params=_params(("parallel",), 40),
        name="pool_decode",
    )(state_t, u_new, w_pool, scale)


def _mix_kernel(x_ref, a_ref, pl_ref, wa_ref, wp_ref, o_ref):
    acc = jnp.dot(a_ref[...], wa_ref[...], preferred_element_type=F32)
    acc = acc + jnp.dot(pl_ref[...], wp_ref[...], preferred_element_type=F32)
    o_ref[...] = x_ref[...] + acc


def _mix(x, a, pooled, w_o, *, tm, tn):
    m, d = x.shape
    aw = a.shape[1]
    pw = pooled.shape[1]
    assert aw == pw
    return pl.pallas_call(
        _mix_kernel,
        out_shape=jax.ShapeDtypeStruct((m, d), F32),
        grid=(m // tm, d // tn),
        in_specs=[
            pl.BlockSpec((tm, tn), lambda i, j: (i, j)),
            pl.BlockSpec((tm, aw), lambda i, j: (i, 0)),
            pl.BlockSpec((tm, pw), lambda i, j: (i, 0)),
            pl.BlockSpec((aw, tn), lambda i, j: (0, j)),
            pl.BlockSpec((pw, tn), lambda i, j: (1, j)),
        ],
        out_specs=pl.BlockSpec((tm, tn), lambda i, j: (i, j)),
        compiler_params=_params(("parallel", "arbitrary"), 48),
        name="mix",
    )(x, a, pooled, w_o, w_o)


def _ffn_kernel(x_ref, g_ref, wg_ref, wu_ref, wd_ref, gf_ref, o_ref, h_ref, *, final_norm):
    f = pl.program_id(1)

    @pl.when(f == 0)
    def _():
        _rmsnorm_rows(x_ref, g_ref, h_ref)
        o_ref[...] = x_ref[...]

    h = h_ref[...]
    gate = jnp.dot(h, wg_ref[...], preferred_element_type=F32)
    up = jnp.dot(h, wu_ref[...], preferred_element_type=F32)
    act = (gate * (1.0 / (1.0 + jnp.exp(-gate))) * up).astype(BF16)
    o_ref[...] += jnp.dot(act, wd_ref[...], preferred_element_type=F32)

    if final_norm:
        @pl.when(f == pl.num_programs(1) - 1)
        def _():
            o_ref[...] = _rmsnorm(o_ref[...], gf_ref[...])


def _ffn(x, g, wg, wu, wd, g_final, *, tm, tf, final_norm):
    m, d = x.shape
    dff = wg.shape[1]
    return pl.pallas_call(
        functools.partial(_ffn_kernel, final_norm=final_norm),
        out_shape=jax.ShapeDtypeStruct((m, d), F32),
        grid=(m // tm, dff // tf),
        in_specs=[
            pl.BlockSpec((tm, d), lambda i, f: (i, 0)),
            pl.BlockSpec((1, d), lambda i, f: (0, 0)),
            pl.BlockSpec((d, tf), lambda i, f: (0, f)),
            pl.BlockSpec((d, tf), lambda i, f: (0, f)),
            pl.BlockSpec((tf, d), lambda i, f: (f, 0)),
            pl.BlockSpec((1, d), lambda i, f: (0, 0)),
        ],
        out_specs=pl.BlockSpec((tm, d), lambda i, f: (i, 0)),
        scratch_shapes=[pltpu.VMEM((tm, d), BF16)],
        compiler_params=_params(("parallel", "arbitrary"), 60),
        name="ffn",
    )(x, g, wg, wu, wd, g_final)


def _decode_mask_is_trailing_window(w, t):
    pos_q = PAST_LEN + np.arange(t)
    pos_k = PAST_LEN - w + np.arange(w + t)
    mask = (pos_k[None, :] <= pos_q[:, None]) & (pos_q[:, None] - pos_k[None, :] < WINDOW)
    want = np.zeros_like(mask)
    want[:, t:] = True
    return bool((mask == want).all())


def kernel(x_prompt, x_sample, cache_k_win, cache_v_win, state_pool, g_mix, w_in, sinks, w_pool, pool_scale,
           w_o, g_ffn, w_gate, w_up, w_down, g_final):
    bsz, seq, d = x_prompt.shape
    db, t_new, _ = x_sample.shape
    depth = w_in.shape[0]
    w_rows = cache_k_win.shape[2]
    assert t_new == 1 and _decode_mask_is_trailing_window(w_rows, t_new)
    pw = min(WINDOW, seq)
    u0 = ATTN_WIDTH + 2 * KV_WIDTH

    xp = x_prompt.reshape(bsz * seq, d)
    xs = x_sample.reshape(db * t_new, d)
    gfin = g_final.reshape(1, d)
    outs = [[] for _ in range(6)]
    for l in range(depth):
        last = l == depth - 1
        w_in_l = w_in[l].astype(BF16)
        w_pool_l = w_pool[l]
        gm, gf = g_mix[l].reshape(1, d), g_ffn[l].reshape(1, d)
        scale = pool_scale[l].reshape(1, -1)

        p = _proj(xp, gm, w_in_l, tm=1024, tn=512, vmem_mib=60)
        p3 = p.reshape(bsz, seq, -1)
        a, wg_l, wu_l, wd_l = _attn_prompt(p3, sinks[l], (w_gate[l], w_up[l], w_down[l]))
        pooled, w_o_l = _pool_prompt(p3, w_pool_l, scale, (w_o[l],))
        x1 = _mix(xp, a.reshape(bsz * seq, -1), pooled.reshape(bsz * seq, -1), w_o_l, tm=1024, tn=512)
        xp = _ffn(x1, gf, wg_l, wu_l, wd_l, gfin, tm=512, tf=256, final_norm=last)
        outs[0].append(p3[:, seq - pw:, ATTN_WIDTH:ATTN_WIDTH + KV_WIDTH].reshape(bsz, pw, N_KV_HEADS, HEAD_DIM))
        outs[1].append(p3[:, seq - pw:, ATTN_WIDTH + KV_WIDTH:u0].reshape(bsz, pw, N_KV_HEADS, HEAD_DIM))
        outs[2].append(p3[:, seq - POOL_STATE_ROWS:, u0:])

        ps = _proj(xs, gm, w_in_l, tm=db, tn=512)
        u_new = ps[:, u0:]
        kt_old = jnp.transpose(cache_k_win[l], (0, 2, 3, 1)).reshape(db, KV_WIDTH, w_rows)
        vt_old = jnp.transpose(cache_v_win[l], (0, 2, 3, 1)).reshape(db, KV_WIDTH, w_rows)
        a_s, kt_win, vt_win = _attn_decode(ps[:, :ATTN_WIDTH].reshape(db, N_HEADS, HEAD_DIM), kt_old, vt_old,
                                           ps[:, ATTN_WIDTH:ATTN_WIDTH + KV_WIDTH], ps[:, ATTN_WIDTH + KV_WIDTH:u0],
                                           sinks[l].reshape(N_HEADS, 1))
        pooled_s, state_t = _pool_decode(jnp.transpose(state_pool[l], (1, 0, 2)), u_new, w_pool_l, scale)
        x1s = _mix(xs, a_s.reshape(db, ATTN_WIDTH).astype(BF16), pooled_s, w_o_l, tm=db, tn=512)
        xs = _ffn(x1s, gf, wg_l, wu_l, wd_l, gfin, tm=db, tf=256, final_norm=last)
        outs[3].append(jnp.transpose(kt_win.reshape(db, N_KV_HEADS, HEAD_DIM, w_rows), (0, 3, 1, 2)))
        outs[4].append(jnp.transpose(vt_win.reshape(db, N_KV_HEADS, HEAD_DIM, w_rows), (0, 3, 1, 2)))
        outs[5].append(jnp.transpose(state_t, (1, 0, 2)))

    return (xp.reshape(bsz, seq, d), xs.reshape(db, t_new, d)) + tuple(jnp.stack(o) for o in outs)
```

```python
import functools

import jax
import jax.numpy as jnp
import numpy as np
from jax import lax
from jax.experimental import pallas as pl
from jax.experimental.pallas import tpu as pltpu

HEAD_DIM = 64
N_KV_HEADS = 4
GROUP = 8
N_HEADS = N_KV_HEADS * GROUP
ATTN_WIDTH = N_HEADS * HEAD_DIM
KV_WIDTH = N_KV_HEADS * HEAD_DIM
WINDOW = 128
BLOCK = 128
PAST_LEN = 8192
POOL_WINDOWS = (2, 4, 8, 16)
POOL_STATE_ROWS = max(POOL_WINDOWS) - 1
NORM_EPS = 1e-5
NEG_INF = -1e30

PROJ_TN = 512
MIX_TN = 512
FFN_TF = 256

BF16 = jnp.bfloat16
F32 = jnp.float32
MIB = 1024 * 1024


def _rmsnorm(xf, g):
    return xf * lax.rsqrt(jnp.mean(xf * xf, axis=-1, keepdims=True) + NORM_EPS) * g


def _params(semantics, vmem_mib):
    return pltpu.CompilerParams(dimension_semantics=semantics, vmem_limit_bytes=vmem_mib * MIB)


def _rmsnorm_rows(src_ref, g_ref, dst_ref, chunk=256):
    rows = src_ref.shape[0]
    chunk = min(chunk, rows)
    for r0 in range(0, rows, chunk):
        dst_ref[r0:r0 + chunk, :] = _rmsnorm(src_ref[r0:r0 + chunk, :], g_ref[...]).astype(dst_ref.dtype)


def _proj_kernel(x_ref, g_ref, w_ref, p_ref, h_ref):
    @pl.when(pl.program_id(1) == 0)
    def _():
        _rmsnorm_rows(x_ref, g_ref, h_ref)

    p_ref[...] = jnp.dot(h_ref[...], w_ref[...], preferred_element_type=F32)


def _proj(x, g, w, *, tm, vmem_mib=48):
    m, d = x.shape
    nj, _, tn = w.shape
    return pl.pallas_call(
        _proj_kernel,
        out_shape=jax.ShapeDtypeStruct((m, nj * tn), F32),
        grid=(m // tm, nj),
        in_specs=[
            pl.BlockSpec((tm, d), lambda i, j: (i, 0)),
            pl.BlockSpec((1, d), lambda i, j: (0, 0)),
            pl.BlockSpec((None, d, tn), lambda i, j: (j, 0, 0)),
        ],
        out_specs=pl.BlockSpec((tm, tn), lambda i, j: (i, j)),
        scratch_shapes=[pltpu.VMEM((tm, d), BF16)],
        compiler_params=_params(("parallel", "arbitrary"), vmem_mib),
        name="proj",
    )(x, g, w)


def _cast_blocks(rows, steps):
    for nblk in range(min(steps, rows // 16), 0, -1):
        if rows % nblk == 0 and (rows // nblk) % 16 == 0:
            return nblk
    raise ValueError(f"no bf16-tileable row split of {rows} rows")


def _cast_job(srcs, tile_w, steps, step_of):
    rows, cols = srcs[0].shape
    nblk = _cast_blocks(rows, steps)
    rb = rows // nblk
    blk = lambda *idx: jnp.minimum(step_of(*idx), nblk - 1)
    in_specs = [pl.BlockSpec((rb, cols), lambda *idx: (blk(*idx), 0)) for _ in srcs]
    if tile_w is None:
        assert len(srcs) == 1
        return (nblk, 1, None), in_specs, in_specs[0], jax.ShapeDtypeStruct((rows, cols), BF16)
    nt, wide = cols // tile_w, tile_w * len(srcs)
    out_spec = pl.BlockSpec((nt, rb, wide), lambda *idx: (0, blk(*idx), 0))
    return (nblk, len(srcs), tile_w), in_specs, out_spec, jax.ShapeDtypeStruct((nt, rows, wide), BF16)


def _run_cast_jobs(step, metas, src_refs, dst_refs):
    src_refs = list(src_refs)
    for (nblk, nsrc, tile_w), dst in zip(metas, dst_refs):
        srcs, src_refs = src_refs[:nsrc], src_refs[nsrc:]

        @pl.when(step < nblk)
        def _():
            if tile_w is None:
                dst[...] = srcs[0][...].astype(dst.dtype)
            else:
                for t in range(dst.shape[0]):
                    for k, src in enumerate(srcs):
                        dst[t, :, k * tile_w:(k + 1) * tile_w] = src[:, t * tile_w:(t + 1) * tile_w].astype(dst.dtype)


def _n_cast_srcs(metas):
    return sum(m[1] for m in metas)


def _sink_softmax(s, sink):
    m = jnp.maximum(jnp.max(s, axis=-1, keepdims=True), sink)
    e = jnp.exp(s - m)
    den = jnp.sum(e, axis=-1, keepdims=True) + jnp.exp(sink - m)
    return e * (1.0 / den)


def _attn_prompt_kernel(*refs, cast_metas):
    nc = _n_cast_srcs(cast_metas)
    sinks_ref, q_ref, kvp_ref, kvc_ref = refs[:4]
    o_ref = refs[4 + nc]
    _run_cast_jobs(pl.program_id(0) * pl.num_programs(1) + pl.program_id(1), cast_metas,
                   refs[4:4 + nc], refs[5 + nc:])
    rows = GROUP * BLOCK
    row = lax.broadcasted_iota(jnp.int32, (rows, BLOCK), 0) % BLOCK
    col = lax.broadcasted_iota(jnp.int32, (rows, BLOCK), 1)
    from_prev = col > row
    scale = HEAD_DIM ** -0.5
    nt_dims = (((1,), (1,)), ((), ()))

    def attend(first_block):
        for kh in range(N_KV_HEADS):
            ks = slice(kh * HEAD_DIM, (kh + 1) * HEAD_DIM)
            vs = slice(KV_WIDTH + kh * HEAD_DIM, KV_WIDTH + (kh + 1) * HEAD_DIM)
            if first_block:
                kmat, vmat = kvc_ref[:, ks].astype(BF16), kvc_ref[:, vs].astype(BF16)
            else:
                kmat = jnp.concatenate([kvp_ref[:, ks], kvc_ref[:, ks]], axis=0).astype(BF16)
                vmat = jnp.concatenate([kvp_ref[:, vs], kvc_ref[:, vs]], axis=0).astype(BF16)
            h0 = kh * GROUP
            qg = jnp.concatenate(
                [(q_ref[:, (h0 + g) * HEAD_DIM:(h0 + g + 1) * HEAD_DIM] * scale).astype(BF16) for g in range(GROUP)],
                axis=0)
            s2 = lax.dot_general(qg, kmat, nt_dims, preferred_element_type=F32)
            if first_block:
                s = jnp.where(from_prev, NEG_INF, s2)
            else:
                s = jnp.where(from_prev, s2[:, :BLOCK], s2[:, BLOCK:])
            sink = jnp.concatenate([jnp.full((BLOCK, BLOCK), sinks_ref[h0 + g], F32) for g in range(GROUP)], axis=0)
            m = jnp.maximum(jnp.max(s, axis=-1, keepdims=True), sink)
            e = jnp.exp(s - m)
            den = jnp.sum(e, axis=-1, keepdims=True) + jnp.exp(sink - m)
            p = e * (1.0 / den)
            if not first_block:
                p = jnp.concatenate([jnp.where(from_prev, p, 0.0), jnp.where(from_prev, 0.0, p)], axis=1)
            og = jnp.dot(p.astype(BF16), vmat, preferred_element_type=F32)
            for gp in range(GROUP // 2):
                c0 = (h0 + 2 * gp) * HEAD_DIM
                o_ref[:, c0:c0 + 2 * HEAD_DIM] = jnp.concatenate(
                    [og[2 * gp * BLOCK:(2 * gp + 1) * BLOCK], og[(2 * gp + 1) * BLOCK:(2 * gp + 2) * BLOCK]],
                    axis=1).astype(o_ref.dtype)

    @pl.when(pl.program_id(1) == 0)
    def _():
        attend(True)

    @pl.when(pl.program_id(1) > 0)
    def _():
        attend(False)


def _attn_prompt(p3, sinks, cast_jobs=()):
    assert WINDOW == BLOCK
    b, s, _ = p3.shape
    nb = s // BLOCK
    kv_blk = ATTN_WIDTH // (2 * KV_WIDTH)
    jobs = [_cast_job(srcs, tw, b * nb, lambda i, n: i * nb + n) for srcs, tw in cast_jobs]
    return pl.pallas_call(
        functools.partial(_attn_prompt_kernel, cast_metas=tuple(j[0] for j in jobs)),
        out_shape=[jax.ShapeDtypeStruct((b, s, ATTN_WIDTH), BF16)] + [j[3] for j in jobs],
        grid=(b, nb),
        in_specs=[
            pl.BlockSpec(memory_space=pltpu.SMEM),
            pl.BlockSpec((None, BLOCK, ATTN_WIDTH), lambda i, n: (i, n, 0)),
            pl.BlockSpec((None, BLOCK, 2 * KV_WIDTH), lambda i, n: (i, jnp.maximum(n - 1, 0), kv_blk)),
            pl.BlockSpec((None, BLOCK, 2 * KV_WIDTH), lambda i, n: (i, n, kv_blk)),
        ] + [sp for j in jobs for sp in j[1]],
        out_specs=[pl.BlockSpec((None, BLOCK, ATTN_WIDTH), lambda i, n: (i, n, 0))] + [j[2] for j in jobs],
        compiler_params=_params(("arbitrary", "arbitrary"), 48),
        name="attn_prompt",
    )(sinks, p3, p3, p3, *[w for srcs, _ in cast_jobs for w in srcs])


def _attn_decode_kernel(sinks_ref, q_ref, kt_ref, vt_ref, kn_ref, vn_ref, o_ref, ktw_ref, vtw_ref):
    bb, _, w = kt_ref.shape
    last = lax.broadcasted_iota(jnp.int32, (KV_WIDTH, w), 1) == w - 1
    for old_ref, new_ref, win_ref in ((kt_ref, kn_ref, ktw_ref), (vt_ref, vn_ref, vtw_ref)):
        for b in range(bb):
            shifted = pltpu.roll(old_ref[b], w - 1, axis=1)
            newest = jnp.broadcast_to(new_ref[:, b:b + 1], (KV_WIDTH, w))
            win_ref[b] = jnp.where(last, newest, shifted)

    scale = HEAD_DIM ** -0.5
    row_kv = lax.broadcasted_iota(jnp.int32, (N_HEADS, KV_WIDTH), 0) // GROUP
    col_kv = lax.broadcasted_iota(jnp.int32, (N_HEADS, KV_WIDTH), 1) // HEAD_DIM
    q = q_ref[...] * scale
    q_bd = jnp.where(row_kv == col_kv, jnp.concatenate([q] * N_KV_HEADS, axis=-1), 0.0).astype(BF16)
    s = lax.dot_general(q_bd, ktw_ref[...].astype(BF16), (((2,), (1,)), ((0,), (0,))),
                        preferred_element_type=F32)
    p = _sink_softmax(s, sinks_ref[...])
    o_full = lax.dot_general(p.astype(BF16), vtw_ref[...].astype(BF16), (((2,), (2,)), ((0,), (0,))),
                             preferred_element_type=F32)
    head_kv = lax.broadcasted_iota(jnp.int32, (N_HEADS, HEAD_DIM), 0) // GROUP
    o = jnp.zeros(o_ref.shape, F32)
    for kh in range(N_KV_HEADS):
        o = o + jnp.where(head_kv == kh, o_full[:, :, kh * HEAD_DIM:(kh + 1) * HEAD_DIM], 0.0)
    o_ref[...] = o


def _attn_decode(q, kt_old, vt_old, k_new, v_new, sinks_col, *, bb=16):
    db = q.shape[0]
    w = kt_old.shape[2]
    nblk = db // bb
    win_spec = pl.BlockSpec((bb, KV_WIDTH, w), lambda i: (i, 0, 0))
    new_spec = pl.BlockSpec((None, KV_WIDTH, bb), lambda i: (i, 0, 0))
    kn, vn = (jnp.transpose(x.reshape(nblk, bb, KV_WIDTH), (0, 2, 1)) for x in (k_new, v_new))
    return pl.pallas_call(
        _attn_decode_kernel,
        out_shape=[jax.ShapeDtypeStruct((db, N_HEADS, HEAD_DIM), F32),
                   jax.ShapeDtypeStruct(kt_old.shape, F32), jax.ShapeDtypeStruct(vt_old.shape, F32)],
        grid=(nblk,),
        in_specs=[
            pl.BlockSpec((N_HEADS, 1), lambda i: (0, 0)),
            pl.BlockSpec((bb, N_HEADS, HEAD_DIM), lambda i: (i, 0, 0)),
            win_spec, win_spec, new_spec, new_spec,
        ],
        out_specs=[pl.BlockSpec((bb, N_HEADS, HEAD_DIM), lambda i: (i, 0, 0)), win_spec, win_spec],
        compiler_params=_params(("parallel",), 32),
        name="attn_decode",
    )(sinks_col, q, kt_old, vt_old, kn, vn)


def _pool_prompt_kernel(*refs, ts, cast_metas):
    ng = len(POOL_WINDOWS)
    nc = _n_cast_srcs(cast_metas)
    u_refs, halo_refs = refs[:ng], refs[ng:2 * ng]
    wp_ref, sc_ref = refs[2 * ng:2 * ng + 2]
    o_ref = refs[2 * ng + 2 + nc]
    ext_ref = refs[-1]
    _run_cast_jobs(pl.program_id(0) * pl.num_programs(1) + pl.program_id(1), cast_metas,
                   refs[2 * ng + 2:2 * ng + 2 + nc], refs[2 * ng + 3 + nc:-1])
    t = pl.program_id(1)
    halo = 16
    pos = t * ts + lax.broadcasted_iota(jnp.int32, (ts, 1), 0)
    gw = wp_ref.shape[-1]
    for g, w in enumerate(POOL_WINDOWS):
        x = u_refs[g][...]
        ext_ref[0:halo, :] = jnp.where(t > 0, halo_refs[g][...], 0.0)
        ext_ref[halo:halo + ts, :] = x
        acc = x
        for k in range(1, w):
            acc = acc + ext_ref[halo - k:halo - k + ts, :]
        cnt = jnp.minimum(pos + 1, w).astype(F32)
        d = acc / cnt - x
        y = jnp.dot(d.astype(BF16), wp_ref[g].astype(BF16), preferred_element_type=F32)
        o_ref[:, g * gw:(g + 1) * gw] = (y * sc_ref[:, g * gw:(g + 1) * gw]).astype(o_ref.dtype)


def _pool_prompt(p3, w_pool, scale, cast_jobs=(), *, ts=512):
    b, s, _ = p3.shape
    nt = s // ts
    jobs = [_cast_job(srcs, tw, b * nt, lambda i, t: i * nt + t) for srcs, tw in cast_jobs]
    ng, gw, _ = w_pool.shape
    u_blk0 = (ATTN_WIDTH + 2 * KV_WIDTH) // gw
    halo = 16
    u_specs = [pl.BlockSpec((None, ts, gw), functools.partial(lambda i, t, g: (i, t, u_blk0 + g), g=g))
               for g in range(ng)]
    halo_specs = [
        pl.BlockSpec((None, halo, gw),
                     functools.partial(lambda i, t, g: (i, jnp.maximum(t * (ts // halo) - 1, 0), u_blk0 + g), g=g))
        for g in range(ng)]
    return pl.pallas_call(
        functools.partial(_pool_prompt_kernel, ts=ts, cast_metas=tuple(j[0] for j in jobs)),
        out_shape=[jax.ShapeDtypeStruct((b, s, ng * gw), BF16)] + [j[3] for j in jobs],
        grid=(b, nt),
        in_specs=u_specs + halo_specs + [
            pl.BlockSpec((ng, gw, gw), lambda i, t: (0, 0, 0)),
            pl.BlockSpec((1, ng * gw), lambda i, t: (0, 0)),
        ] + [sp for j in jobs for sp in j[1]],
        out_specs=[pl.BlockSpec((None, ts, ng * gw), lambda i, t: (i, t, 0))] + [j[2] for j in jobs],
        scratch_shapes=[pltpu.VMEM((halo + ts, gw), F32)],
        compiler_params=_params(("arbitrary", "arbitrary"), 48),
        name="pool_prompt",
    )(*([p3] * (2 * ng)), w_pool, scale, *[w for srcs, _ in cast_jobs for w in srcs])


def _pool_decode_kernel(st_ref, u_ref, wp_ref, sc_ref, o_ref, new_st_ref):
    gw = wp_ref.shape[-1]
    rows = st_ref.shape[0]
    for r in range(rows - 1):
        new_st_ref[r] = st_ref[r + 1]
    new_st_ref[rows - 1] = u_ref[...]
    for g, w in enumerate(POOL_WINDOWS):
        cs = slice(g * gw, (g + 1) * gw)
        x = u_ref[:, cs]
        acc = x
        for k in range(1, w):
            acc = acc + st_ref[rows - k, :, cs]
        d = acc / float(min(PAST_LEN + 1, w)) - x
        y = jnp.dot(d.astype(BF16), wp_ref[g].astype(BF16), preferred_element_type=F32)
        o_ref[:, cs] = (y * sc_ref[:, cs]).astype(o_ref.dtype)


def _pool_decode(state_t, u_new, w_pool, scale, *, bb=32):
    rows, db, pw = state_t.shape
    ng, gw, _ = w_pool.shape
    st_spec = pl.BlockSpec((rows, bb, pw), lambda i: (0, i, 0))
    return pl.pallas_call(
        _pool_decode_kernel,
        out_shape=[jax.ShapeDtypeStruct((db, pw), BF16), jax.ShapeDtypeStruct(state_t.shape, F32)],
        grid=(db // bb,),
        in_specs=[
            st_spec,
            pl.BlockSpec((bb, pw), lambda i: (i, 0)),
            pl.BlockSpec((ng, gw, gw), lambda i: (0, 0, 0)),
            pl.BlockSpec((1, pw), lambda i: (0, 0)),
        ],
        out_specs=[pl.BlockSpec((bb, pw), lambda i: (i, 0)), st_spec],
        compiler_params=_params(("parallel",), 40),
        name="pool_decode",
    )(state_t, u_new, w_pool, scale)


def _mix_kernel(*refs, cast_metas):
    nc = _n_cast_srcs(cast_metas)
    x_ref, a_ref, pl_ref, wa_ref, wp_ref = refs[:5]
    o_ref = refs[5 + nc]
    _run_cast_jobs(pl.program_id(0) * pl.num_programs(1) + pl.program_id(1), cast_metas,
                   refs[5:5 + nc], refs[6 + nc:])
    acc = jnp.dot(a_ref[...], wa_ref[...], preferred_element_type=F32)
    acc = acc + jnp.dot(pl_ref[...], wp_ref[...], preferred_element_type=F32)
    o_ref[...] = x_ref[...] + acc


def _mix(x, a, pooled, w_o, cast_jobs=(), *, tm, vmem_mib=48):
    m, d = x.shape
    aw = a.shape[1]
    pw = pooled.shape[1]
    nj, _, tn = w_o.shape
    assert aw == pw
    jobs = [_cast_job(srcs, tw, (m // tm) * nj, lambda i, j: i * nj + j) for srcs, tw in cast_jobs]
    return pl.pallas_call(
        functools.partial(_mix_kernel, cast_metas=tuple(j[0] for j in jobs)),
        out_shape=[jax.ShapeDtypeStruct((m, d), F32)] + [j[3] for j in jobs],
        grid=(m // tm, nj),
        in_specs=[
            pl.BlockSpec((tm, tn), lambda i, j: (i, j)),
            pl.BlockSpec((tm, aw), lambda i, j: (i, 0)),
            pl.BlockSpec((tm, pw), lambda i, j: (i, 0)),
            pl.BlockSpec((None, aw, tn), lambda i, j: (j, 0, 0)),
            pl.BlockSpec((None, pw, tn), lambda i, j: (j, 1, 0)),
        ] + [sp for j in jobs for sp in j[1]],
        out_specs=[pl.BlockSpec((tm, tn), lambda i, j: (i, j))] + [j[2] for j in jobs],
        compiler_params=_params(("arbitrary", "arbitrary"), vmem_mib),
        name="mix",
    )(x, a, pooled, w_o, w_o, *[w for srcs, _ in cast_jobs for w in srcs])


def _ffn_kernel(x_ref, g_ref, wgu_ref, wd_ref, gf_ref, o_ref, h_ref, *, final_norm):
    f = pl.program_id(1)

    @pl.when(f == 0)
    def _():
        _rmsnorm_rows(x_ref, g_ref, h_ref)
        o_ref[...] = x_ref[...]

    tf = wd_ref.shape[0]
    gate_up = jnp.dot(h_ref[...], wgu_ref[...], preferred_element_type=F32)
    gate, up = gate_up[:, :tf], gate_up[:, tf:]
    act = (gate * (1.0 / (1.0 + jnp.exp(-gate))) * up).astype(BF16)
    o_ref[...] += jnp.dot(act, wd_ref[...], preferred_element_type=F32)

    if final_norm:
        @pl.when(f == pl.num_programs(1) - 1)
        def _():
            _rmsnorm_rows(o_ref, gf_ref, o_ref)


def _ffn(x, g, wgu, wd, g_final, *, tm, final_norm):
    m, d = x.shape
    nf, _, tf2 = wgu.shape
    tf = tf2 // 2
    return pl.pallas_call(
        functools.partial(_ffn_kernel, final_norm=final_norm),
        out_shape=jax.ShapeDtypeStruct((m, d), F32),
        grid=(m // tm, nf),
        in_specs=[
            pl.BlockSpec((tm, d), lambda i, f: (i, 0)),
            pl.BlockSpec((1, d), lambda i, f: (0, 0)),
            pl.BlockSpec((None, d, tf2), lambda i, f: (f, 0, 0)),
            pl.BlockSpec((tf, d), lambda i, f: (f, 0)),
            pl.BlockSpec((1, d), lambda i, f: (0, 0)),
        ],
        out_specs=pl.BlockSpec((tm, d), lambda i, f: (i, 0)),
        scratch_shapes=[pltpu.VMEM((tm, d), BF16)],
        compiler_params=_params(("parallel", "arbitrary"), 60),
        name="ffn",
    )(x, g, wgu, wd, g_final)


def _decode_mask_is_trailing_window(w, t):
    pos_q = PAST_LEN + np.arange(t)
    pos_k = PAST_LEN - w + np.arange(w + t)
    mask = (pos_k[None, :] <= pos_q[:, None]) & (pos_q[:, None] - pos_k[None, :] < WINDOW)
    want = np.zeros_like(mask)
    want[:, t:] = True
    return bool((mask == want).all())


def kernel(x_prompt, x_sample, cache_k_win, cache_v_win, state_pool, g_mix, w_in, sinks, w_pool, pool_scale,
           w_o, g_ffn, w_gate, w_up, w_down, g_final):
    bsz, seq, d = x_prompt.shape
    db, t_new, _ = x_sample.shape
    depth = w_in.shape[0]
    w_rows = cache_k_win.shape[2]
    assert t_new == 1 and _decode_mask_is_trailing_window(w_rows, t_new)
    pw = min(WINDOW, seq)
    u0 = ATTN_WIDTH + 2 * KV_WIDTH

    xp = x_prompt.reshape(bsz * seq, d)
    xs = x_sample.reshape(db * t_new, d)
    gfin = g_final.reshape(1, d)
    outs = [[] for _ in range(6)]
    for l in range(depth):
        last = l == depth - 1
        n_in = w_in.shape[2]
        w_in_l = jnp.transpose(w_in[l].astype(BF16).reshape(d, n_in // PROJ_TN, PROJ_TN), (1, 0, 2))
        w_pool_l = w_pool[l]
        gm, gf = g_mix[l].reshape(1, d), g_ffn[l].reshape(1, d)
        scale = pool_scale[l].reshape(1, -1)

        p = _proj(xp, gm, w_in_l, tm=1024, vmem_mib=60)
        p3 = p.reshape(bsz, seq, -1)
        a, wgu_l = _attn_prompt(p3, sinks[l], [((w_gate[l], w_up[l]), FFN_TF)])
        pooled, w_o_l = _pool_prompt(p3, w_pool_l, scale, [((w_o[l],), MIX_TN)])
        x1, wd_l = _mix(xp, a.reshape(bsz * seq, -1), pooled.reshape(bsz * seq, -1), w_o_l,
                        [((w_down[l],), None)], tm=1024, vmem_mib=56)
        xp = _ffn(x1, gf, wgu_l, wd_l, gfin, tm=512, final_norm=last)
        outs[0].append(p3[:, seq - pw:, ATTN_WIDTH:ATTN_WIDTH + KV_WIDTH].reshape(bsz, pw, N_KV_HEADS, HEAD_DIM))
        outs[1].append(p3[:, seq - pw:, ATTN_WIDTH + KV_WIDTH:u0].reshape(bsz, pw, N_KV_HEADS, HEAD_DIM))
        outs[2].append(p3[:, seq - POOL_STATE_ROWS:, u0:])

        ps = _proj(xs, gm, w_in_l, tm=db)
        u_new = ps[:, u0:]
        kt_old = jnp.transpose(cache_k_win[l], (0, 2, 3, 1)).reshape(db, KV_WIDTH, w_rows)
        vt_old = jnp.transpose(cache_v_win[l], (0, 2, 3, 1)).reshape(db, KV_WIDTH, w_rows)
        a_s, kt_win, vt_win = _attn_decode(ps[:, :ATTN_WIDTH].reshape(db, N_HEADS, HEAD_DIM), kt_old, vt_old,
                                           ps[:, ATTN_WIDTH:ATTN_WIDTH + KV_WIDTH], ps[:, ATTN_WIDTH + KV_WIDTH:u0],
                                           sinks[l].reshape(N_HEADS, 1))
        pooled_s, state_t = _pool_decode(jnp.transpose(state_pool[l], (1, 0, 2)), u_new, w_pool_l, scale)
        x1s, = _mix(xs, a_s.reshape(db, ATTN_WIDTH).astype(BF16), pooled_s, w_o_l, tm=db)
        xs = _ffn(x1s, gf, wgu_l, wd_l, gfin, tm=db, final_norm=last)
        outs[3].append(jnp.transpose(kt_win.reshape(db, N_KV_HEADS, HEAD_DIM, w_rows), (0, 3, 1, 2)))
        outs[4].append(jnp.transpose(vt_win.reshape(db, N_KV_HEADS, HEAD_DIM, w_rows), (0, 3, 1, 2)))
        outs[5].append(jnp.transpose(state_t, (1, 0, 2)))

    return (xp.reshape(bsz, seq, d), xs.reshape(db, t_new, d)) + tuple(jnp.stack(o) for o in outs)
```

```python
import functools

import jax
import jax.numpy as jnp
import numpy as np
from jax import lax
from jax.experimental import pallas as pl
from jax.experimental.pallas import tpu as pltpu

HEAD_DIM = 64
N_KV_HEADS = 4
GROUP = 8
N_HEADS = N_KV_HEADS * GROUP
ATTN_WIDTH = N_HEADS * HEAD_DIM
KV_WIDTH = N_KV_HEADS * HEAD_DIM
WINDOW = 128
BLOCK = 128
PAST_LEN = 8192
POOL_WINDOWS = (2, 4, 8, 16)
POOL_STATE_ROWS = max(POOL_WINDOWS) - 1
NORM_EPS = 1e-5
NEG_INF = -1e30

PROJ_TN = 512
MIX_TN = 512
FFN_TF = 256

BF16 = jnp.bfloat16
F32 = jnp.float32
MIB = 1024 * 1024


def _rmsnorm(xf, g):
    return xf * lax.rsqrt(jnp.mean(xf * xf, axis=-1, keepdims=True) + NORM_EPS) * g


def _params(semantics, vmem_mib):
    return pltpu.CompilerParams(dimension_semantics=semantics, vmem_limit_bytes=vmem_mib * MIB)


def _rmsnorm_rows(src_ref, g_ref, dst_ref, chunk=256):
    rows = src_ref.shape[0]
    chunk = min(chunk, rows)
    for r0 in range(0, rows, chunk):
        dst_ref[r0:r0 + chunk, :] = _rmsnorm(src_ref[r0:r0 + chunk, :], g_ref[...]).astype(dst_ref.dtype)


def _proj_kernel(x_ref, g_ref, w_ref, p_ref, h_ref):
    @pl.when(pl.program_id(1) == 0)
    def _():
        _rmsnorm_rows(x_ref, g_ref, h_ref)

    p_ref[...] = jnp.dot(h_ref[...], w_ref[...], preferred_element_type=F32)


def _proj(x, g, w, *, tm, tn=PROJ_TN, vmem_mib=48):
    m, d = x.shape
    n = w.shape[1]
    return pl.pallas_call(
        _proj_kernel,
        out_shape=jax.ShapeDtypeStruct((m, n), F32),
        grid=(m // tm, n // tn),
        in_specs=[
            pl.BlockSpec((tm, d), lambda i, j: (i, 0)),
            pl.BlockSpec((1, d), lambda i, j: (0, 0)),
            pl.BlockSpec((d, tn), lambda i, j: (0, j)),
        ],
        out_specs=pl.BlockSpec((tm, tn), lambda i, j: (i, j)),
        scratch_shapes=[pltpu.VMEM((tm, d), BF16)],
        compiler_params=_params(("parallel", "arbitrary"), vmem_mib),
        name="proj",
    )(x, g, w)


def _cast_blocks(rows, steps):
    for nblk in range(min(steps, rows // 16), 0, -1):
        if rows % nblk == 0 and (rows // nblk) % 16 == 0:
            return nblk
    raise ValueError(f"no bf16-tileable row split of {rows} rows")


def _cast_job(srcs, tile_w, steps, step_of):
    rows, cols = srcs[0].shape
    nblk = _cast_blocks(rows, steps)
    rb = rows // nblk
    blk = lambda *idx: jnp.minimum(step_of(*idx), nblk - 1)
    in_specs = [pl.BlockSpec((rb, cols), lambda *idx: (blk(*idx), 0)) for _ in srcs]
    if tile_w is None:
        assert len(srcs) == 1
        return (nblk, 1, None), in_specs, in_specs[0], jax.ShapeDtypeStruct((rows, cols), BF16)
    nt, wide = cols // tile_w, tile_w * len(srcs)
    out_spec = pl.BlockSpec((nt, rb, wide), lambda *idx: (0, blk(*idx), 0))
    return (nblk, len(srcs), tile_w), in_specs, out_spec, jax.ShapeDtypeStruct((nt, rows, wide), BF16)


def _run_cast_jobs(step, metas, src_refs, dst_refs):
    src_refs = list(src_refs)
    for (nblk, nsrc, tile_w), dst in zip(metas, dst_refs):
        srcs, src_refs = src_refs[:nsrc], src_refs[nsrc:]

        @pl.when(step < nblk)
        def _():
            if tile_w is None:
                dst[...] = srcs[0][...].astype(dst.dtype)
            else:
                for t in range(dst.shape[0]):
                    for k, src in enumerate(srcs):
                        dst[t, :, k * tile_w:(k + 1) * tile_w] = src[:, t * tile_w:(t + 1) * tile_w].astype(dst.dtype)


def _n_cast_srcs(metas):
    return sum(m[1] for m in metas)


def _sink_softmax(s, sink):
    m = jnp.maximum(jnp.max(s, axis=-1, keepdims=True), sink)
    e = jnp.exp(s - m)
    den = jnp.sum(e, axis=-1, keepdims=True) + jnp.exp(sink - m)
    return e * (1.0 / den)


def _attn_prompt_kernel(*refs, cast_metas):
    nc = _n_cast_srcs(cast_metas)
    sinks_ref, q_ref, kvp_ref, kvc_ref = refs[:4]
    o_ref = refs[4 + nc]
    _run_cast_jobs(pl.program_id(0) * pl.num_programs(1) + pl.program_id(1), cast_metas,
                   refs[4:4 + nc], refs[5 + nc:])
    rows = GROUP * BLOCK
    row = lax.broadcasted_iota(jnp.int32, (rows, BLOCK), 0) % BLOCK
    col = lax.broadcasted_iota(jnp.int32, (rows, BLOCK), 1)
    from_prev = col > row
    scale = HEAD_DIM ** -0.5
    nt_dims = (((1,), (1,)), ((), ()))

    def attend(first_block):
        for kh in range(N_KV_HEADS):
            ks = slice(kh * HEAD_DIM, (kh + 1) * HEAD_DIM)
            vs = slice(KV_WIDTH + kh * HEAD_DIM, KV_WIDTH + (kh + 1) * HEAD_DIM)
            if first_block:
                kmat, vmat = kvc_ref[:, ks].astype(BF16), kvc_ref[:, vs].astype(BF16)
            else:
                kmat = jnp.concatenate([kvp_ref[:, ks], kvc_ref[:, ks]], axis=0).astype(BF16)
                vmat = jnp.concatenate([kvp_ref[:, vs], kvc_ref[:, vs]], axis=0).astype(BF16)
            h0 = kh * GROUP
            qg = jnp.concatenate(
                [(q_ref[:, (h0 + g) * HEAD_DIM:(h0 + g + 1) * HEAD_DIM] * scale).astype(BF16) for g in range(GROUP)],
                axis=0)
            s2 = lax.dot_general(qg, kmat, nt_dims, preferred_element_type=F32)
            if first_block:
                s = jnp.where(from_prev, NEG_INF, s2)
            else:
                s = jnp.where(from_prev, s2[:, :BLOCK], s2[:, BLOCK:])
            sink = jnp.concatenate([jnp.full((BLOCK, BLOCK), sinks_ref[h0 + g], F32) for g in range(GROUP)], axis=0)
            m = jnp.maximum(jnp.max(s, axis=-1, keepdims=True), sink)
            e = jnp.exp(s - m)
            den = jnp.sum(e, axis=-1, keepdims=True) + jnp.exp(sink - m)
            p = e * (1.0 / den)
            if not first_block:
                p = jnp.concatenate([jnp.where(from_prev, p, 0.0), jnp.where(from_prev, 0.0, p)], axis=1)
            og = jnp.dot(p.astype(BF16), vmat, preferred_element_type=F32)
            for gp in range(GROUP // 2):
                c0 = (h0 + 2 * gp) * HEAD_DIM
                o_ref[:, c0:c0 + 2 * HEAD_DIM] = jnp.concatenate(
                    [og[2 * gp * BLOCK:(2 * gp + 1) * BLOCK], og[(2 * gp + 1) * BLOCK:(2 * gp + 2) * BLOCK]],
                    axis=1).astype(o_ref.dtype)

    @pl.when(pl.program_id(1) == 0)
    def _():
        attend(True)

    @pl.when(pl.program_id(1) > 0)
    def _():
        attend(False)


def _attn_prompt(p3, sinks, cast_jobs=()):
    assert WINDOW == BLOCK
    b, s, _ = p3.shape
    nb = s // BLOCK
    kv_blk = ATTN_WIDTH // (2 * KV_WIDTH)
    jobs = [_cast_job(srcs, tw, b * nb, lambda i, n: i * nb + n) for srcs, tw in cast_jobs]
    return pl.pallas_call(
        functools.partial(_attn_prompt_kernel, cast_metas=tuple(j[0] for j in jobs)),
        out_shape=[jax.ShapeDtypeStruct((b, s, ATTN_WIDTH), BF16)] + [j[3] for j in jobs],
        grid=(b, nb),
        in_specs=[
            pl.BlockSpec(memory_space=pltpu.SMEM),
            pl.BlockSpec((None, BLOCK, ATTN_WIDTH), lambda i, n: (i, n, 0)),
            pl.BlockSpec((None, BLOCK, 2 * KV_WIDTH), lambda i, n: (i, jnp.maximum(n - 1, 0), kv_blk)),
            pl.BlockSpec((None, BLOCK, 2 * KV_WIDTH), lambda i, n: (i, n, kv_blk)),
        ] + [sp for j in jobs for sp in j[1]],
        out_specs=[pl.BlockSpec((None, BLOCK, ATTN_WIDTH), lambda i, n: (i, n, 0))] + [j[2] for j in jobs],
        compiler_params=_params(("arbitrary", "arbitrary"), 48),
        name="attn_prompt",
    )(sinks, p3, p3, p3, *[w for srcs, _ in cast_jobs for w in srcs])


def _attn_decode_kernel(sinks_ref, q_ref, kt_ref, vt_ref, kn_ref, vn_ref, o_ref, ktw_ref, vtw_ref):
    bb, _, w = kt_ref.shape
    last = lax.broadcasted_iota(jnp.int32, (KV_WIDTH, w), 1) == w - 1
    for old_ref, new_ref, win_ref in ((kt_ref, kn_ref, ktw_ref), (vt_ref, vn_ref, vtw_ref)):
        for b in range(bb):
            shifted = pltpu.roll(old_ref[b], w - 1, axis=1)
            newest = jnp.broadcast_to(new_ref[:, b:b + 1], (KV_WIDTH, w))
            win_ref[b] = jnp.where(last, newest, shifted)

    scale = HEAD_DIM ** -0.5
    row_kv = lax.broadcasted_iota(jnp.int32, (N_HEADS, KV_WIDTH), 0) // GROUP
    col_kv = lax.broadcasted_iota(jnp.int32, (N_HEADS, KV_WIDTH), 1) // HEAD_DIM
    q = q_ref[...] * scale
    q_bd = jnp.where(row_kv == col_kv, jnp.concatenate([q] * N_KV_HEADS, axis=-1), 0.0).astype(BF16)
    s = lax.dot_general(q_bd, ktw_ref[...].astype(BF16), (((2,), (1,)), ((0,), (0,))),
                        preferred_element_type=F32)
    p = _sink_softmax(s, sinks_ref[...])
    o_full = lax.dot_general(p.astype(BF16), vtw_ref[...].astype(BF16), (((2,), (2,)), ((0,), (0,))),
                             preferred_element_type=F32)
    head_kv = lax.broadcasted_iota(jnp.int32, (N_HEADS, HEAD_DIM), 0) // GROUP
    o = jnp.zeros(o_ref.shape, F32)
    for kh in range(N_KV_HEADS):
        o = o + jnp.where(head_kv == kh, o_full[:, :, kh * HEAD_DIM:(kh + 1) * HEAD_DIM], 0.0)
    o_ref[...] = o


def _attn_decode(q, kt_old, vt_old, k_new, v_new, sinks_col, *, bb=16):
    db = q.shape[0]
    w = kt_old.shape[2]
    nblk = db // bb
    win_spec = pl.BlockSpec((bb, KV_WIDTH, w), lambda i: (i, 0, 0))
    new_spec = pl.BlockSpec((None, KV_WIDTH, bb), lambda i: (i, 0, 0))
    kn, vn = (jnp.transpose(x.reshape(nblk, bb, KV_WIDTH), (0, 2, 1)) for x in (k_new, v_new))
    return pl.pallas_call(
        _attn_decode_kernel,
        out_shape=[jax.ShapeDtypeStruct((db, N_HEADS, HEAD_DIM), F32),
                   jax.ShapeDtypeStruct(kt_old.shape, F32), jax.ShapeDtypeStruct(vt_old.shape, F32)],
        grid=(nblk,),
        in_specs=[
            pl.BlockSpec((N_HEADS, 1), lambda i: (0, 0)),
            pl.BlockSpec((bb, N_HEADS, HEAD_DIM), lambda i: (i, 0, 0)),
            win_spec, win_spec, new_spec, new_spec,
        ],
        out_specs=[pl.BlockSpec((bb, N_HEADS, HEAD_DIM), lambda i: (i, 0, 0)), win_spec, win_spec],
        compiler_params=_params(("parallel",), 32),
        name="attn_decode",
    )(sinks_col, q, kt_old, vt_old, kn, vn)


def _pool_prompt_kernel(*refs, ts, cast_metas):
    ng = len(POOL_WINDOWS)
    nc = _n_cast_srcs(cast_metas)
    u_refs, halo_refs = refs[:ng], refs[ng:2 * ng]
    wp_ref, sc_ref = refs[2 * ng:2 * ng + 2]
    o_ref = refs[2 * ng + 2 + nc]
    ext_ref = refs[-1]
    _run_cast_jobs(pl.program_id(0) * pl.num_programs(1) + pl.program_id(1), cast_metas,
                   refs[2 * ng + 2:2 * ng + 2 + nc], refs[2 * ng + 3 + nc:-1])
    t = pl.program_id(1)
    halo = 16
    pos = t * ts + lax.broadcasted_iota(jnp.int32, (ts, 1), 0)
    gw = wp_ref.shape[-1]
    for g, w in enumerate(POOL_WINDOWS):
        x = u_refs[g][...]
        ext_ref[0:halo, :] = jnp.where(t > 0, halo_refs[g][...], 0.0)
        ext_ref[halo:halo + ts, :] = x
        acc = x
        for k in range(1, w):
            acc = acc + ext_ref[halo - k:halo - k + ts, :]
        cnt = jnp.minimum(pos + 1, w).astype(F32)
        d = acc / cnt - x
        y = jnp.dot(d.astype(BF16), wp_ref[g].astype(BF16), preferred_element_type=F32)
        o_ref[:, g * gw:(g + 1) * gw] = (y * sc_ref[:, g * gw:(g + 1) * gw]).astype(o_ref.dtype)


def _pool_prompt(p3, w_pool, scale, cast_jobs=(), *, ts=512):
    b, s, _ = p3.shape
    nt = s // ts
    jobs = [_cast_job(srcs, tw, b * nt, lambda i, t: i * nt + t) for srcs, tw in cast_jobs]
    ng, gw, _ = w_pool.shape
    u_blk0 = (ATTN_WIDTH + 2 * KV_WIDTH) // gw
    halo = 16
    u_specs = [pl.BlockSpec((None, ts, gw), functools.partial(lambda i, t, g: (i, t, u_blk0 + g), g=g))
               for g in range(ng)]
    halo_specs = [
        pl.BlockSpec((None, halo, gw),
                     functools.partial(lambda i, t, g: (i, jnp.maximum(t * (ts // halo) - 1, 0), u_blk0 + g), g=g))
        for g in range(ng)]
    return pl.pallas_call(
        functools.partial(_pool_prompt_kernel, ts=ts, cast_metas=tuple(j[0] for j in jobs)),
        out_shape=[jax.ShapeDtypeStruct((b, s, ng * gw), BF16)] + [j[3] for j in jobs],
        grid=(b, nt),
        in_specs=u_specs + halo_specs + [
            pl.BlockSpec((ng, gw, gw), lambda i, t: (0, 0, 0)),
            pl.BlockSpec((1, ng * gw), lambda i, t: (0, 0)),
        ] + [sp for j in jobs for sp in j[1]],
        out_specs=[pl.BlockSpec((None, ts, ng * gw), lambda i, t: (i, t, 0))] + [j[2] for j in jobs],
        scratch_shapes=[pltpu.VMEM((halo + ts, gw), F32)],
        compiler_params=_params(("arbitrary", "arbitrary"), 48),
        name="pool_prompt",
    )(*([p3] * (2 * ng)), w_pool, scale, *[w for srcs, _ in cast_jobs for w in srcs])


def _pool_decode_kernel(st_ref, u_ref, wp_ref, sc_ref, o_ref, new_st_ref):
    gw = wp_ref.shape[-1]
    rows = st_ref.shape[0]
    for r in range(rows - 1):
        new_st_ref[r] = st_ref[r + 1]
    new_st_ref[rows - 1] = u_ref[...]
    for g, w in enumerate(POOL_WINDOWS):
        cs = slice(g * gw, (g + 1) * gw)
        x = u_ref[:, cs]
        acc = x
        for k in range(1, w):
            acc = acc + st_ref[rows - k, :, cs]
        d = acc / float(min(PAST_LEN + 1, w)) - x
        y = jnp.dot(d.astype(BF16), wp_ref[g].astype(BF16), preferred_element_type=F32)
        o_ref[:, cs] = (y * sc_ref[:, cs]).astype(o_ref.dtype)


def _pool_decode(state_t, u_new, w_pool, scale, *, bb=32):
    rows, db, pw = state_t.shape
    ng, gw, _ = w_pool.shape
    st_spec = pl.BlockSpec((rows, bb, pw), lambda i: (0, i, 0))
    return pl.pallas_call(
        _pool_decode_kernel,
        out_shape=[jax.ShapeDtypeStruct((db, pw), BF16), jax.ShapeDtypeStruct(state_t.shape, F32)],
        grid=(db // bb,),
        in_specs=[
            st_spec,
            pl.BlockSpec((bb, pw), lambda i: (i, 0)),
            pl.BlockSpec((ng, gw, gw), lambda i: (0, 0, 0)),
            pl.BlockSpec((1, pw), lambda i: (0, 0)),
        ],
        out_specs=[pl.BlockSpec((bb, pw), lambda i: (i, 0)), st_spec],
        compiler_params=_params(("parallel",), 40),
        name="pool_decode",
    )(state_t, u_new, w_pool, scale)


def _mix_kernel(*refs, cast_metas):
    nc = _n_cast_srcs(cast_metas)
    x_ref, a_ref, pl_ref, wa_ref, wp_ref = refs[:5]
    o_ref = refs[5 + nc]
    _run_cast_jobs(pl.program_id(0) * pl.num_programs(1) + pl.program_id(1), cast_metas,
                   refs[5:5 + nc], refs[6 + nc:])
    acc = jnp.dot(a_ref[...], wa_ref[...], preferred_element_type=F32)
    acc = acc + jnp.dot(pl_ref[...], wp_ref[...], preferred_element_type=F32)
    o_ref[...] = x_ref[...] + acc


def _mix(x, a, pooled, w_o, cast_jobs=(), *, tm, vmem_mib=48):
    m, d = x.shape
    aw = a.shape[1]
    pw = pooled.shape[1]
    nj, _, tn = w_o.shape
    assert aw == pw
    jobs = [_cast_job(srcs, tw, (m // tm) * nj, lambda i, j: i * nj + j) for srcs, tw in cast_jobs]
    return pl.pallas_call(
        functools.partial(_mix_kernel, cast_metas=tuple(j[0] for j in jobs)),
        out_shape=[jax.ShapeDtypeStruct((m, d), F32)] + [j[3] for j in jobs],
        grid=(m // tm, nj),
        in_specs=[
            pl.BlockSpec((tm, tn), lambda i, j: (i, j)),
            pl.BlockSpec((tm, aw), lambda i, j: (i, 0)),
            pl.BlockSpec((tm, pw), lambda i, j: (i, 0)),
            pl.BlockSpec((None, aw, tn), lambda i, j: (j, 0, 0)),
            pl.BlockSpec((None, pw, tn), lambda i, j: (j, 1, 0)),
        ] + [sp for j in jobs for sp in j[1]],
        out_specs=[pl.BlockSpec((tm, tn), lambda i, j: (i, j))] + [j[2] for j in jobs],
        compiler_params=_params(("arbitrary", "arbitrary"), vmem_mib),
        name="mix",
    )(x, a, pooled, w_o, w_o, *[w for srcs, _ in cast_jobs for w in srcs])


def _ffn_kernel(x_ref, g_ref, wgu_ref, wd_ref, gf_ref, o_ref, h_ref, *, final_norm):
    f = pl.program_id(1)

    @pl.when(f == 0)
    def _():
        _rmsnorm_rows(x_ref, g_ref, h_ref)
        o_ref[...] = x_ref[...]

    tf = wd_ref.shape[0]
    gate_up = jnp.dot(h_ref[...], wgu_ref[...], preferred_element_type=F32)
    gate, up = gate_up[:, :tf], gate_up[:, tf:]
    act = (gate * (1.0 / (1.0 + jnp.exp(-gate))) * up).astype(BF16)
    o_ref[...] += jnp.dot(act, wd_ref[...], preferred_element_type=F32)

    if final_norm:
        @pl.when(f == pl.num_programs(1) - 1)
        def _():
            _rmsnorm_rows(o_ref, gf_ref, o_ref)


def _ffn(x, g, wgu, wd, g_final, *, tm, final_norm):
    m, d = x.shape
    nf, _, tf2 = wgu.shape
    tf = tf2 // 2
    return pl.pallas_call(
        functools.partial(_ffn_kernel, final_norm=final_norm),
        out_shape=jax.ShapeDtypeStruct((m, d), F32),
        grid=(m // tm, nf),
        in_specs=[
            pl.BlockSpec((tm, d), lambda i, f: (i, 0)),
            pl.BlockSpec((1, d), lambda i, f: (0, 0)),
            pl.BlockSpec((None, d, tf2), lambda i, f: (f, 0, 0)),
            pl.BlockSpec((tf, d), lambda i, f: (f, 0)),
            pl.BlockSpec((1, d), lambda i, f: (0, 0)),
        ],
        out_specs=pl.BlockSpec((tm, d), lambda i, f: (i, 0)),
        scratch_shapes=[pltpu.VMEM((tm, d), BF16)],
        compiler_params=_params(("parallel", "arbitrary"), 60),
        name="ffn",
    )(x, g, wgu, wd, g_final)


def _decode_mask_is_trailing_window(w, t):
    pos_q = PAST_LEN + np.arange(t)
    pos_k = PAST_LEN - w + np.arange(w + t)
    mask = (pos_k[None, :] <= pos_q[:, None]) & (pos_q[:, None] - pos_k[None, :] < WINDOW)
    want = np.zeros_like(mask)
    want[:, t:] = True
    return bool((mask == want).all())


def kernel(x_prompt, x_sample, cache_k_win, cache_v_win, state_pool, g_mix, w_in, sinks, w_pool, pool_scale,
           w_o, g_ffn, w_gate, w_up, w_down, g_final):
    bsz, seq, d = x_prompt.shape
    db, t_new, _ = x_sample.shape
    depth = w_in.shape[0]
    w_rows = cache_k_win.shape[2]
    assert t_new == 1 and _decode_mask_is_trailing_window(w_rows, t_new)
    pw = min(WINDOW, seq)
    u0 = ATTN_WIDTH + 2 * KV_WIDTH

    xp = x_prompt.reshape(bsz * seq, d)
    xs = x_sample.reshape(db * t_new, d)
    gfin = g_final.reshape(1, d)
    outs = [[] for _ in range(6)]
    for l in range(depth):
        last = l == depth - 1
        w_in_l = w_in[l].astype(BF16)
        w_pool_l = w_pool[l]
        gm, gf = g_mix[l].reshape(1, d), g_ffn[l].reshape(1, d)
        scale = pool_scale[l].reshape(1, -1)

        p = _proj(xp, gm, w_in_l, tm=1024, vmem_mib=60)
        p3 = p.reshape(bsz, seq, -1)
        a, wgu_l = _attn_prompt(p3, sinks[l], [((w_gate[l], w_up[l]), FFN_TF)])
        pooled, w_o_l = _pool_prompt(p3, w_pool_l, scale, [((w_o[l],), MIX_TN)])
        x1, wd_l = _mix(xp, a.reshape(bsz * seq, -1), pooled.reshape(bsz * seq, -1), w_o_l,
                        [((w_down[l],), None)], tm=1024, vmem_mib=56)
        xp = _ffn(x1, gf, wgu_l, wd_l, gfin, tm=512, final_norm=last)
        outs[0].append(p3[:, seq - pw:, ATTN_WIDTH:ATTN_WIDTH + KV_WIDTH].reshape(bsz, pw, N_KV_HEADS, HEAD_DIM))
        outs[1].append(p3[:, seq - pw:, ATTN_WIDTH + KV_WIDTH:u0].reshape(bsz, pw, N_KV_HEADS, HEAD_DIM))
        outs[2].append(p3[:, seq - POOL_STATE_ROWS:, u0:])

        ps = _proj(xs, gm, w_in_l, tm=db)
        u_new = ps[:, u0:]
        kt_old = jnp.transpose(cache_k_win[l], (0, 2, 3, 1)).reshape(db, KV_WIDTH, w_rows)
        vt_old = jnp.transpose(cache_v_win[l], (0, 2, 3, 1)).reshape(db, KV_WIDTH, w_rows)
        a_s, kt_win, vt_win = _attn_decode(ps[:, :ATTN_WIDTH].reshape(db, N_HEADS, HEAD_DIM), kt_old, vt_old,
                                           ps[:, ATTN_WIDTH:ATTN_WIDTH + KV_WIDTH], ps[:, ATTN_WIDTH + KV_WIDTH:u0],
                                           sinks[l].reshape(N_HEADS, 1))
        pooled_s, state_t = _pool_decode(jnp.transpose(state_pool[l], (1, 0, 2)), u_new, w_pool_l, scale)
        x1s, = _mix(xs, a_s.reshape(db, ATTN_WIDTH).astype(BF16), pooled_s, w_o_l, tm=db)
        xs = _ffn(x1s, gf, wgu_l, wd_l, gfin, tm=db, final_norm=last)
        outs[3].append(jnp.transpose(kt_win.reshape(db, N_KV_HEADS, HEAD_DIM, w_rows), (0, 3, 1, 2)))
        outs[4].append(jnp.transpose(vt_win.reshape(db, N_KV_HEADS, HEAD_DIM, w_rows), (0, 3, 1, 2)))
        outs[5].append(jnp.transpose(state_t, (1, 0, 2)))

    return (xp.reshape(bsz, seq, d), xs.reshape(db, t_new, d)) + tuple(jnp.stack(o) for o in outs)
```

```python
import functools

import jax
import jax.numpy as jnp
import numpy as np
from jax import lax
from jax.experimental import pallas as pl
from jax.experimental.pallas import tpu as pltpu

HEAD_DIM = 64
N_KV_HEADS = 4
GROUP = 8
N_HEADS = N_KV_HEADS * GROUP
ATTN_WIDTH = N_HEADS * HEAD_DIM
KV_WIDTH = N_KV_HEADS * HEAD_DIM
WINDOW = 128
BLOCK = 128
PAST_LEN = 8192
POOL_WINDOWS = (2, 4, 8, 16)
POOL_STATE_ROWS = max(POOL_WINDOWS) - 1
NORM_EPS = 1e-5
NEG_INF = -1e30

PROJ_TN = 512
MIX_TN = 512
FFN_TF = 256
FFN_TILES_PER_STEP = (1, 2)

BF16 = jnp.bfloat16
F32 = jnp.float32
MIB = 1024 * 1024


def _rmsnorm(xf, g):
    return xf * lax.rsqrt(jnp.mean(xf * xf, axis=-1, keepdims=True) + NORM_EPS) * g


def _params(semantics, vmem_mib):
    return pltpu.CompilerParams(dimension_semantics=semantics, vmem_limit_bytes=vmem_mib * MIB)


def _rmsnorm_rows(src_ref, g_ref, dst_ref, chunk=256):
    rows = src_ref.shape[0]
    chunk = min(chunk, rows)
    for r0 in range(0, rows, chunk):
        dst_ref[r0:r0 + chunk, :] = _rmsnorm(src_ref[r0:r0 + chunk, :], g_ref[...]).astype(dst_ref.dtype)


def _proj_kernel(x_ref, g_ref, w_ref, p_ref, h_ref):
    @pl.when(pl.program_id(1) == 0)
    def _():
        _rmsnorm_rows(x_ref, g_ref, h_ref)

    p_ref[...] = jnp.dot(h_ref[...], w_ref[...], preferred_element_type=F32)


def _proj(x, g, w, *, tm, tn=PROJ_TN, vmem_mib=48):
    m, d = x.shape
    n = w.shape[1]
    return pl.pallas_call(
        _proj_kernel,
        out_shape=jax.ShapeDtypeStruct((m, n), F32),
        grid=(m // tm, n // tn),
        in_specs=[
            pl.BlockSpec((tm, d), lambda i, j: (i, 0)),
            pl.BlockSpec((1, d), lambda i, j: (0, 0)),
            pl.BlockSpec((d, tn), lambda i, j: (0, j)),
        ],
        out_specs=pl.BlockSpec((tm, tn), lambda i, j: (i, j)),
        scratch_shapes=[pltpu.VMEM((tm, d), BF16)],
        compiler_params=_params(("parallel", "arbitrary"), vmem_mib),
        name="proj",
    )(x, g, w)


def _cast_blocks(rows, steps):
    for nblk in range(min(steps, rows // 16), 0, -1):
        if rows % nblk == 0 and (rows // nblk) % 16 == 0:
            return nblk
    raise ValueError(f"no bf16-tileable row split of {rows} rows")


def _cast_job(srcs, tile_w, pad, steps, step_of):
    rows, cols = srcs[0].shape
    nblk = _cast_blocks(rows, steps)
    rb = rows // nblk
    in_specs = [pl.BlockSpec((rb, cols), lambda *idx: (jnp.minimum(step_of(*idx), nblk - 1), 0)) for _ in srcs]
    if tile_w is None:
        assert len(srcs) == 1 and nblk + pad <= steps
        out_spec = pl.BlockSpec((rb, cols), lambda *idx: (jnp.minimum(step_of(*idx), nblk + pad - 1), 0))
        return (nblk, 1, None, pad), in_specs, out_spec, jax.ShapeDtypeStruct((rows + pad * rb, cols), BF16)
    nt, wide = cols // tile_w, tile_w * len(srcs)
    out_spec = pl.BlockSpec((nt + pad, rb, wide), lambda *idx: (0, jnp.minimum(step_of(*idx), nblk - 1), 0))
    return (nblk, len(srcs), tile_w, pad), in_specs, out_spec, jax.ShapeDtypeStruct((nt + pad, rows, wide), BF16)


def _run_cast_jobs(step, metas, src_refs, dst_refs):
    src_refs = list(src_refs)
    for (nblk, nsrc, tile_w, pad), dst in zip(metas, dst_refs):
        srcs, src_refs = src_refs[:nsrc], src_refs[nsrc:]

        @pl.when(step < nblk)
        def _():
            if tile_w is None:
                dst[...] = srcs[0][...].astype(dst.dtype)
            else:
                nt = dst.shape[0] - pad
                for t in range(nt):
                    for k, src in enumerate(srcs):
                        dst[t, :, k * tile_w:(k + 1) * tile_w] = src[:, t * tile_w:(t + 1) * tile_w].astype(dst.dtype)
                for t in range(nt, nt + pad):
                    dst[t] = jnp.zeros(dst.shape[1:], dst.dtype)

        if tile_w is None and pad:
            @pl.when((step >= nblk) & (step < nblk + pad))
            def _():
                dst[...] = jnp.zeros(dst.shape, dst.dtype)


def _n_cast_srcs(metas):
    return sum(m[1] for m in metas)


def _sink_softmax(s, sink):
    m = jnp.maximum(jnp.max(s, axis=-1, keepdims=True), sink)
    e = jnp.exp(s - m)
    den = jnp.sum(e, axis=-1, keepdims=True) + jnp.exp(sink - m)
    return e * (1.0 / den)


def _attn_prompt_kernel(*refs, cast_metas):
    nc = _n_cast_srcs(cast_metas)
    sinks_ref, q_ref, kvp_ref, kvc_ref = refs[:4]
    o_ref = refs[4 + nc]
    _run_cast_jobs(pl.program_id(0) * pl.num_programs(1) + pl.program_id(1), cast_metas,
                   refs[4:4 + nc], refs[5 + nc:])
    rows = GROUP * BLOCK
    row = lax.broadcasted_iota(jnp.int32, (rows, BLOCK), 0) % BLOCK
    col = lax.broadcasted_iota(jnp.int32, (rows, BLOCK), 1)
    from_prev = col > row
    scale = HEAD_DIM ** -0.5
    nt_dims = (((1,), (1,)), ((), ()))

    def attend(first_block):
        for kh in range(N_KV_HEADS):
            ks = slice(kh * HEAD_DIM, (kh + 1) * HEAD_DIM)
            vs = slice(KV_WIDTH + kh * HEAD_DIM, KV_WIDTH + (kh + 1) * HEAD_DIM)
            if first_block:
                kmat, vmat = kvc_ref[:, ks].astype(BF16), kvc_ref[:, vs].astype(BF16)
            else:
                kmat = jnp.concatenate([kvp_ref[:, ks], kvc_ref[:, ks]], axis=0).astype(BF16)
                vmat = jnp.concatenate([kvp_ref[:, vs], kvc_ref[:, vs]], axis=0).astype(BF16)
            h0 = kh * GROUP
            qg = jnp.concatenate(
                [(q_ref[:, (h0 + g) * HEAD_DIM:(h0 + g + 1) * HEAD_DIM] * scale).astype(BF16) for g in range(GROUP)],
                axis=0)
            s2 = lax.dot_general(qg, kmat, nt_dims, preferred_element_type=F32)
            if first_block:
                s = jnp.where(from_prev, NEG_INF, s2)
            else:
                s = jnp.where(from_prev, s2[:, :BLOCK], s2[:, BLOCK:])
            sink = jnp.concatenate([jnp.full((BLOCK, BLOCK), sinks_ref[h0 + g], F32) for g in range(GROUP)], axis=0)
            m = jnp.maximum(jnp.max(s, axis=-1, keepdims=True), sink)
            e = jnp.exp(s - m)
            den = jnp.sum(e, axis=-1, keepdims=True) + jnp.exp(sink - m)
            p = e * (1.0 / den)
            if not first_block:
                p = jnp.concatenate([jnp.where(from_prev, p, 0.0), jnp.where(from_prev, 0.0, p)], axis=1)
            og = jnp.dot(p.astype(BF16), vmat, preferred_element_type=F32)
            for gp in range(GROUP // 2):
                c0 = (h0 + 2 * gp) * HEAD_DIM
                o_ref[:, c0:c0 + 2 * HEAD_DIM] = jnp.concatenate(
                    [og[2 * gp * BLOCK:(2 * gp + 1) * BLOCK], og[(2 * gp + 1) * BLOCK:(2 * gp + 2) * BLOCK]],
                    axis=1).astype(o_ref.dtype)

    @pl.when(pl.program_id(1) == 0)
    def _():
        attend(True)

    @pl.when(pl.program_id(1) > 0)
    def _():
        attend(False)


def _attn_prompt(p3, sinks, cast_jobs=()):
    assert WINDOW == BLOCK
    b, s, _ = p3.shape
    nb = s // BLOCK
    kv_blk = ATTN_WIDTH // (2 * KV_WIDTH)
    jobs = [_cast_job(*job, b * nb, lambda i, n: i * nb + n) for job in cast_jobs]
    return pl.pallas_call(
        functools.partial(_attn_prompt_kernel, cast_metas=tuple(j[0] for j in jobs)),
        out_shape=[jax.ShapeDtypeStruct((b, s, ATTN_WIDTH), BF16)] + [j[3] for j in jobs],
        grid=(b, nb),
        in_specs=[
            pl.BlockSpec(memory_space=pltpu.SMEM),
            pl.BlockSpec((None, BLOCK, ATTN_WIDTH), lambda i, n: (i, n, 0)),
            pl.BlockSpec((None, BLOCK, 2 * KV_WIDTH), lambda i, n: (i, jnp.maximum(n - 1, 0), kv_blk)),
            pl.BlockSpec((None, BLOCK, 2 * KV_WIDTH), lambda i, n: (i, n, kv_blk)),
        ] + [sp for j in jobs for sp in j[1]],
        out_specs=[pl.BlockSpec((None, BLOCK, ATTN_WIDTH), lambda i, n: (i, n, 0))] + [j[2] for j in jobs],
        compiler_params=_params(("arbitrary", "arbitrary"), 48),
        name="attn_prompt",
    )(sinks, p3, p3, p3, *[w for job in cast_jobs for w in job[0]])


def _attn_decode_kernel(sinks_ref, q_ref, kt_ref, vt_ref, kn_ref, vn_ref, o_ref, ktw_ref, vtw_ref):
    bb, _, w = kt_ref.shape
    last = lax.broadcasted_iota(jnp.int32, (KV_WIDTH, w), 1) == w - 1
    for old_ref, new_ref, win_ref in ((kt_ref, kn_ref, ktw_ref), (vt_ref, vn_ref, vtw_ref)):
        for b in range(bb):
            shifted = pltpu.roll(old_ref[b], w - 1, axis=1)
            newest = jnp.broadcast_to(new_ref[:, b:b + 1], (KV_WIDTH, w))
            win_ref[b] = jnp.where(last, newest, shifted)

    scale = HEAD_DIM ** -0.5
    row_kv = lax.broadcasted_iota(jnp.int32, (N_HEADS, KV_WIDTH), 0) // GROUP
    col_kv = lax.broadcasted_iota(jnp.int32, (N_HEADS, KV_WIDTH), 1) // HEAD_DIM
    q = q_ref[...] * scale
    q_bd = jnp.where(row_kv == col_kv, jnp.concatenate([q] * N_KV_HEADS, axis=-1), 0.0).astype(BF16)
    s = lax.dot_general(q_bd, ktw_ref[...].astype(BF16), (((2,), (1,)), ((0,), (0,))),
                        preferred_element_type=F32)
    p = _sink_softmax(s, sinks_ref[...])
    o_full = lax.dot_general(p.astype(BF16), vtw_ref[...].astype(BF16), (((2,), (2,)), ((0,), (0,))),
                             preferred_element_type=F32)
    head_kv = lax.broadcasted_iota(jnp.int32, (N_HEADS, HEAD_DIM), 0) // GROUP
    o = jnp.zeros(o_ref.shape, F32)
    for kh in range(N_KV_HEADS):
        o = o + jnp.where(head_kv == kh, o_full[:, :, kh * HEAD_DIM:(kh + 1) * HEAD_DIM], 0.0)
    o_ref[...] = o


def _attn_decode(q, kt_old, vt_old, k_new, v_new, sinks_col, *, bb=16):
    db = q.shape[0]
    w = kt_old.shape[2]
    nblk = db // bb
    win_spec = pl.BlockSpec((bb, KV_WIDTH, w), lambda i: (i, 0, 0))
    new_spec = pl.BlockSpec((None, KV_WIDTH, bb), lambda i: (i, 0, 0))
    kn, vn = (jnp.transpose(x.reshape(nblk, bb, KV_WIDTH), (0, 2, 1)) for x in (k_new, v_new))
    return pl.pallas_call(
        _attn_decode_kernel,
        out_shape=[jax.ShapeDtypeStruct((db, N_HEADS, HEAD_DIM), F32),
                   jax.ShapeDtypeStruct(kt_old.shape, F32), jax.ShapeDtypeStruct(vt_old.shape, F32)],
        grid=(nblk,),
        in_specs=[
            pl.BlockSpec((N_HEADS, 1), lambda i: (0, 0)),
            pl.BlockSpec((bb, N_HEADS, HEAD_DIM), lambda i: (i, 0, 0)),
            win_spec, win_spec, new_spec, new_spec,
        ],
        out_specs=[pl.BlockSpec((bb, N_HEADS, HEAD_DIM), lambda i: (i, 0, 0)), win_spec, win_spec],
        compiler_params=_params(("parallel",), 32),
        name="attn_decode",
    )(sinks_col, q, kt_old, vt_old, kn, vn)


def _pool_prompt_kernel(*refs, ts, cast_metas):
    ng = len(POOL_WINDOWS)
    nc = _n_cast_srcs(cast_metas)
    u_refs, halo_refs = refs[:ng], refs[ng:2 * ng]
    wp_ref, sc_ref = refs[2 * ng:2 * ng + 2]
    o_ref = refs[2 * ng + 2 + nc]
    ext_ref = refs[-1]
    _run_cast_jobs(pl.program_id(0) * pl.num_programs(1) + pl.program_id(1), cast_metas,
                   refs[2 * ng + 2:2 * ng + 2 + nc], refs[2 * ng + 3 + nc:-1])
    t = pl.program_id(1)
    halo = 16
    pos = t * ts + lax.broadcasted_iota(jnp.int32, (ts, 1), 0)
    gw = wp_ref.shape[-1]
    for g, w in enumerate(POOL_WINDOWS):
        x = u_refs[g][...]
        ext_ref[0:halo, :] = jnp.where(t > 0, halo_refs[g][...], 0.0)
        ext_ref[halo:halo + ts, :] = x
        acc = x
        for k in range(1, w):
            acc = acc + ext_ref[halo - k:halo - k + ts, :]
        cnt = jnp.minimum(pos + 1, w).astype(F32)
        d = acc / cnt - x
        y = jnp.dot(d.astype(BF16), wp_ref[g].astype(BF16), preferred_element_type=F32)
        o_ref[:, g * gw:(g + 1) * gw] = (y * sc_ref[:, g * gw:(g + 1) * gw]).astype(o_ref.dtype)


def _pool_prompt(p3, w_pool, scale, cast_jobs=(), *, ts=512):
    b, s, _ = p3.shape
    nt = s // ts
    jobs = [_cast_job(*job, b * nt, lambda i, t: i * nt + t) for job in cast_jobs]
    ng, gw, _ = w_pool.shape
    u_blk0 = (ATTN_WIDTH + 2 * KV_WIDTH) // gw
    halo = 16
    u_specs = [pl.BlockSpec((None, ts, gw), functools.partial(lambda i, t, g: (i, t, u_blk0 + g), g=g))
               for g in range(ng)]
    halo_specs = [
        pl.BlockSpec((None, halo, gw),
                     functools.partial(lambda i, t, g: (i, jnp.maximum(t * (ts // halo) - 1, 0), u_blk0 + g), g=g))
        for g in range(ng)]
    return pl.pallas_call(
        functools.partial(_pool_prompt_kernel, ts=ts, cast_metas=tuple(j[0] for j in jobs)),
        out_shape=[jax.ShapeDtypeStruct((b, s, ng * gw), BF16)] + [j[3] for j in jobs],
        grid=(b, nt),
        in_specs=u_specs + halo_specs + [
            pl.BlockSpec((ng, gw, gw), lambda i, t: (0, 0, 0)),
            pl.BlockSpec((1, ng * gw), lambda i, t: (0, 0)),
        ] + [sp for j in jobs for sp in j[1]],
        out_specs=[pl.BlockSpec((None, ts, ng * gw), lambda i, t: (i, t, 0))] + [j[2] for j in jobs],
        scratch_shapes=[pltpu.VMEM((halo + ts, gw), F32)],
        compiler_params=_params(("arbitrary", "arbitrary"), 48),
        name="pool_prompt",
    )(*([p3] * (2 * ng)), w_pool, scale, *[w for job in cast_jobs for w in job[0]])


def _pool_decode_kernel(st_ref, u_ref, wp_ref, sc_ref, o_ref, new_st_ref):
    gw = wp_ref.shape[-1]
    rows = st_ref.shape[0]
    for r in range(rows - 1):
        new_st_ref[r] = st_ref[r + 1]
    new_st_ref[rows - 1] = u_ref[...]
    for g, w in enumerate(POOL_WINDOWS):
        cs = slice(g * gw, (g + 1) * gw)
        x = u_ref[:, cs]
        acc = x
        for k in range(1, w):
            acc = acc + st_ref[rows - k, :, cs]
        d = acc / float(min(PAST_LEN + 1, w)) - x
        y = jnp.dot(d.astype(BF16), wp_ref[g].astype(BF16), preferred_element_type=F32)
        o_ref[:, cs] = (y * sc_ref[:, cs]).astype(o_ref.dtype)


def _pool_decode(state_t, u_new, w_pool, scale, *, bb=32):
    rows, db, pw = state_t.shape
    ng, gw, _ = w_pool.shape
    st_spec = pl.BlockSpec((rows, bb, pw), lambda i: (0, i, 0))
    return pl.pallas_call(
        _pool_decode_kernel,
        out_shape=[jax.ShapeDtypeStruct((db, pw), BF16), jax.ShapeDtypeStruct(state_t.shape, F32)],
        grid=(db // bb,),
        in_specs=[
            st_spec,
            pl.BlockSpec((bb, pw), lambda i: (i, 0)),
            pl.BlockSpec((ng, gw, gw), lambda i: (0, 0, 0)),
            pl.BlockSpec((1, pw), lambda i: (0, 0)),
        ],
        out_specs=[pl.BlockSpec((bb, pw), lambda i: (i, 0)), st_spec],
        compiler_params=_params(("parallel",), 40),
        name="pool_decode",
    )(state_t, u_new, w_pool, scale)


def _mix_kernel(*refs, cast_metas):
    nc = _n_cast_srcs(cast_metas)
    x_ref, a_ref, pl_ref, wa_ref, wp_ref = refs[:5]
    o_ref = refs[5 + nc]
    _run_cast_jobs(pl.program_id(0) * pl.num_programs(1) + pl.program_id(1), cast_metas,
                   refs[5:5 + nc], refs[6 + nc:])
    acc = jnp.dot(a_ref[...], wa_ref[...], preferred_element_type=F32)
    acc = acc + jnp.dot(pl_ref[...], wp_ref[...], preferred_element_type=F32)
    o_ref[...] = x_ref[...] + acc


def _mix(x, a, pooled, w_o, cast_jobs=(), *, tm, vmem_mib=48):
    m, d = x.shape
    aw = a.shape[1]
    pw = pooled.shape[1]
    nj, _, tn = w_o.shape
    assert aw == pw
    jobs = [_cast_job(*job, (m // tm) * nj, lambda i, j: i * nj + j) for job in cast_jobs]
    return pl.pallas_call(
        functools.partial(_mix_kernel, cast_metas=tuple(j[0] for j in jobs)),
        out_shape=[jax.ShapeDtypeStruct((m, d), F32)] + [j[3] for j in jobs],
        grid=(m // tm, nj),
        in_specs=[
            pl.BlockSpec((tm, tn), lambda i, j: (i, j)),
            pl.BlockSpec((tm, aw), lambda i, j: (i, 0)),
            pl.BlockSpec((tm, pw), lambda i, j: (i, 0)),
            pl.BlockSpec((None, aw, tn), lambda i, j: (j, 0, 0)),
            pl.BlockSpec((None, pw, tn), lambda i, j: (j, 1, 0)),
        ] + [sp for j in jobs for sp in j[1]],
        out_specs=[pl.BlockSpec((tm, tn), lambda i, j: (i, j))] + [j[2] for j in jobs],
        compiler_params=_params(("arbitrary", "arbitrary"), vmem_mib),
        name="mix",
    )(x, a, pooled, w_o, w_o, *[w for job in cast_jobs for w in job[0]])


def _ffn_tile(h, wgu, wd):
    tf = wd.shape[0]
    gate_up = jnp.dot(h, wgu, preferred_element_type=F32)
    gate, up = gate_up[:, :tf], gate_up[:, tf:]
    act = (gate * (1.0 / (1.0 + jnp.exp(-gate))) * up).astype(BF16)
    return jnp.dot(act, wd, preferred_element_type=F32)


def _ffn_kernel(x_hbm, g_ref, wgu_hbm, wd_hbm, gf_ref, y_hbm, h_ref, *, grid, specs, n_tiles, final_norm):
    del x_hbm
    per_step = specs[1].block_shape[0]
    tf = specs[2].block_shape[0] // per_step
    tail = n_tiles % per_step

    def step(acc_ref, wgu_ref, wd_ref):
        f = pl.program_id(1)

        def add_tiles(count):
            for k in range(count):
                acc_ref[...] += _ffn_tile(h_ref[...], wgu_ref[k], wd_ref[k * tf:(k + 1) * tf, :])

        @pl.when(f == 0)
        def _():
            _rmsnorm_rows(acc_ref, g_ref, h_ref)

        if tail:
            pl.when(f < grid[1] - 1)(lambda: add_tiles(per_step))
            pl.when(f == grid[1] - 1)(lambda: add_tiles(tail))
        else:
            add_tiles(per_step)

        if final_norm:
            @pl.when(f == grid[1] - 1)
            def _():
                _rmsnorm_rows(acc_ref, gf_ref, acc_ref)

    buffers = [
        pltpu.BufferedRef.input_output(specs[0], F32, 2),
        pltpu.BufferedRef.input(specs[1], BF16, 2),
        pltpu.BufferedRef.input(specs[2], BF16, 2),
    ]
    pipeline = pltpu.emit_pipeline(step, grid=grid, in_specs=specs, out_specs=[])
    pl.run_scoped(lambda bufs: pipeline(y_hbm, wgu_hbm, wd_hbm, allocations=bufs), buffers)


def _ffn(x, g, wgu, wd, g_final, *, tm, per_step, n_tiles, final_norm):
    m, d = x.shape
    nt, _, tf2 = wgu.shape
    grid = (m // tm, -(-n_tiles // per_step))
    assert nt >= grid[1] * per_step and wd.shape[0] == nt * tf2 // 2
    specs = [
        pl.BlockSpec((tm, d), lambda i, f: (i, 0)),
        pl.BlockSpec((per_step, d, tf2), lambda i, f: (f, 0, 0)),
        pl.BlockSpec((per_step * tf2 // 2, d), lambda i, f: (f, 0)),
    ]
    whole = pl.BlockSpec(memory_space=pltpu.VMEM)
    hbm = pl.BlockSpec(memory_space=pl.ANY)
    return pl.pallas_call(
        functools.partial(_ffn_kernel, grid=grid, specs=specs, n_tiles=n_tiles, final_norm=final_norm),
        out_shape=jax.ShapeDtypeStruct((m, d), F32),
        in_specs=[hbm, whole, hbm, hbm, whole],
        out_specs=hbm,
        input_output_aliases={0: 0},
        scratch_shapes=[pltpu.VMEM((tm, d), BF16)],
        compiler_params=pltpu.CompilerParams(vmem_limit_bytes=60 * MIB),
        name="ffn",
    )(x, g, wgu, wd, g_final)


def _decode_mask_is_trailing_window(w, t):
    pos_q = PAST_LEN + np.arange(t)
    pos_k = PAST_LEN - w + np.arange(w + t)
    mask = (pos_k[None, :] <= pos_q[:, None]) & (pos_q[:, None] - pos_k[None, :] < WINDOW)
    want = np.zeros_like(mask)
    want[:, t:] = True
    return bool((mask == want).all())


def kernel(x_prompt, x_sample, cache_k_win, cache_v_win, state_pool, g_mix, w_in, sinks, w_pool, pool_scale,
           w_o, g_ffn, w_gate, w_up, w_down, g_final):
    bsz, seq, d = x_prompt.shape
    db, t_new, _ = x_sample.shape
    depth = w_in.shape[0]
    w_rows = cache_k_win.shape[2]
    assert t_new == 1 and _decode_mask_is_trailing_window(w_rows, t_new)
    pw = min(WINDOW, seq)
    u0 = ATTN_WIDTH + 2 * KV_WIDTH

    xp = x_prompt.reshape(bsz * seq, d)
    xs = x_sample.reshape(db * t_new, d)
    gfin = g_final.reshape(1, d)
    outs = [[] for _ in range(6)]
    for l in range(depth):
        last = l == depth - 1
        w_in_l = w_in[l].astype(BF16)
        w_pool_l = w_pool[l]
        gm, gf = g_mix[l].reshape(1, d), g_ffn[l].reshape(1, d)
        scale = pool_scale[l].reshape(1, -1)

        p = _proj(xp, gm, w_in_l, tm=1024, vmem_mib=60)
        p3 = p.reshape(bsz, seq, -1)
        n_tiles = w_gate.shape[2] // FFN_TF
        ffn_pad = -n_tiles % max(FFN_TILES_PER_STEP)
        a, wgu_l = _attn_prompt(p3, sinks[l], [((w_gate[l], w_up[l]), FFN_TF, ffn_pad)])
        pooled, w_o_l = _pool_prompt(p3, w_pool_l, scale, [((w_o[l],), MIX_TN, 0)])
        x1, wd_l = _mix(xp, a.reshape(bsz * seq, -1), pooled.reshape(bsz * seq, -1), w_o_l,
                        [((w_down[l],), None, ffn_pad)], tm=1024, vmem_mib=56)
        xp = _ffn(x1, gf, wgu_l, wd_l, gfin, tm=1024, per_step=FFN_TILES_PER_STEP[0], n_tiles=n_tiles,
                  final_norm=last)
        outs[0].append(p3[:, seq - pw:, ATTN_WIDTH:ATTN_WIDTH + KV_WIDTH].reshape(bsz, pw, N_KV_HEADS, HEAD_DIM))
        outs[1].append(p3[:, seq - pw:, ATTN_WIDTH + KV_WIDTH:u0].reshape(bsz, pw, N_KV_HEADS, HEAD_DIM))
        outs[2].append(p3[:, seq - POOL_STATE_ROWS:, u0:])

        ps = _proj(xs, gm, w_in_l, tm=db)
        u_new = ps[:, u0:]
        kt_old = jnp.transpose(cache_k_win[l], (0, 2, 3, 1)).reshape(db, KV_WIDTH, w_rows)
        vt_old = jnp.transpose(cache_v_win[l], (0, 2, 3, 1)).reshape(db, KV_WIDTH, w_rows)
        a_s, kt_win, vt_win = _attn_decode(ps[:, :ATTN_WIDTH].reshape(db, N_HEADS, HEAD_DIM), kt_old, vt_old,
                                           ps[:, ATTN_WIDTH:ATTN_WIDTH + KV_WIDTH], ps[:, ATTN_WIDTH + KV_WIDTH:u0],
                                           sinks[l].reshape(N_HEADS, 1))
        pooled_s, state_t = _pool_decode(jnp.transpose(state_pool[l], (1, 0, 2)), u_new, w_pool_l, scale)
        x1s, = _mix(xs, a_s.reshape(db, ATTN_WIDTH).astype(BF16), pooled_s, w_o_l, tm=db)
        xs = _ffn(x1s, gf, wgu_l, wd_l, gfin, tm=db, per_step=FFN_TILES_PER_STEP[1], n_tiles=n_tiles,
                  final_norm=last)
        outs[3].append(jnp.transpose(kt_win.reshape(db, N_KV_HEADS, HEAD_DIM, w_rows), (0, 3, 1, 2)))
        outs[4].append(jnp.transpose(vt_win.reshape(db, N_KV_HEADS, HEAD_DIM, w_rows), (0, 3, 1, 2)))
        outs[5].append(jnp.transpose(state_t, (1, 0, 2)))

    return (xp.reshape(bsz, seq, d), xs.reshape(db, t_new, d)) + tuple(jnp.stack(o) for o in outs)
```

```python
import functools

import jax
import jax.numpy as jnp
import numpy as np
from jax import lax
from jax.experimental import pallas as pl
from jax.experimental.pallas import tpu as pltpu

HEAD_DIM = 64
N_KV_HEADS = 4
GROUP = 8
N_HEADS = N_KV_HEADS * GROUP
ATTN_WIDTH = N_HEADS * HEAD_DIM
KV_WIDTH = N_KV_HEADS * HEAD_DIM
WINDOW = 128
BLOCK = 128
PAST_LEN = 8192
POOL_WINDOWS = (2, 4, 8, 16)
POOL_STATE_ROWS = max(POOL_WINDOWS) - 1
NORM_EPS = 1e-5
NEG_INF = -1e30

PROJ_TN = 512
MIX_TN = 512
FFN_TF = 256
FFN_TILES_PER_STEP = (1, 2)

BF16 = jnp.bfloat16
F32 = jnp.float32
MIB = 1024 * 1024


def _rmsnorm(xf, g):
    return xf * lax.rsqrt(jnp.mean(xf * xf, axis=-1, keepdims=True) + NORM_EPS) * g


def _params(semantics, vmem_mib):
    return pltpu.CompilerParams(dimension_semantics=semantics, vmem_limit_bytes=vmem_mib * MIB)


def _rmsnorm_rows(src_ref, g_ref, dst_ref, chunk=256):
    rows = src_ref.shape[0]
    chunk = min(chunk, rows)
    for r0 in range(0, rows, chunk):
        dst_ref[r0:r0 + chunk, :] = _rmsnorm(src_ref[r0:r0 + chunk, :], g_ref[...]).astype(dst_ref.dtype)


def _proj_kernel(x_hbm, g_ref, w_hbm, p_hbm, h_ref, *, grid, specs):
    def step(x_ref, w_ref, p_ref):
        @pl.when(pl.program_id(1) == 0)
        def _():
            _rmsnorm_rows(x_ref, g_ref, h_ref)

        p_ref[...] = jnp.dot(h_ref[...], w_ref[...], preferred_element_type=F32)

    pltpu.emit_pipeline(step, grid=grid, in_specs=specs[:2], out_specs=specs[2:])(x_hbm, w_hbm, p_hbm)


def _early(n_row_tiles):
    return pl.Buffered(2, use_lookahead=True) if n_row_tiles > 1 else None


def _proj(x, g, w, *, tm, tn=PROJ_TN, w_buffers=2, vmem_mib=48):
    m, d = x.shape
    n = w.shape[1]
    grid = (m // tm, n // tn)
    specs = [
        pl.BlockSpec((tm, d), lambda i, j: (i, 0), pipeline_mode=_early(grid[0])),
        pl.BlockSpec((d, tn), lambda i, j: (0, j), pipeline_mode=pl.Buffered(w_buffers)),
        pl.BlockSpec((tm, tn), lambda i, j: (i, j)),
    ]
    whole = pl.BlockSpec(memory_space=pltpu.VMEM)
    hbm = pl.BlockSpec(memory_space=pl.ANY)
    return pl.pallas_call(
        functools.partial(_proj_kernel, grid=grid, specs=specs),
        out_shape=jax.ShapeDtypeStruct((m, n), F32),
        in_specs=[hbm, whole, hbm],
        out_specs=hbm,
        scratch_shapes=[pltpu.VMEM((tm, d), BF16)],
        compiler_params=pltpu.CompilerParams(vmem_limit_bytes=vmem_mib * MIB),
        name="proj",
    )(x, g, w)


def _cast_blocks(rows, steps):
    for nblk in range(min(steps, rows // 16), 0, -1):
        if rows % nblk == 0 and (rows // nblk) % 16 == 0:
            return nblk
    raise ValueError(f"no bf16-tileable row split of {rows} rows")


def _cast_job(srcs, tile_w, pad, steps, step_of):
    rows, cols = srcs[0].shape
    nblk = _cast_blocks(rows, steps)
    rb = rows // nblk
    in_specs = [pl.BlockSpec((rb, cols), lambda *idx: (jnp.minimum(step_of(*idx), nblk - 1), 0)) for _ in srcs]
    if tile_w is None:
        assert len(srcs) == 1 and nblk + pad <= steps
        out_spec = pl.BlockSpec((rb, cols), lambda *idx: (jnp.minimum(step_of(*idx), nblk + pad - 1), 0))
        return (nblk, 1, None, pad, False), in_specs, out_spec, jax.ShapeDtypeStruct((rows + pad * rb, cols), BF16)
    nt, wide = cols // tile_w, tile_w * len(srcs)
    out_spec = pl.BlockSpec((nt + pad, rb, wide), lambda *idx: (0, jnp.minimum(step_of(*idx), nblk - 1), 0))
    meta = (nblk, len(srcs), tile_w, pad, nblk == steps)
    return meta, in_specs, out_spec, jax.ShapeDtypeStruct((nt + pad, rows, wide), BF16)


def _run_cast_jobs(step, metas, src_refs, dst_refs):
    src_refs = list(src_refs)
    for (nblk, nsrc, tile_w, pad, every_step), dst in zip(metas, dst_refs):
        srcs, src_refs = src_refs[:nsrc], src_refs[nsrc:]

        @(lambda body: body()) if every_step else pl.when(step < nblk)
        def _():
            if tile_w is None:
                dst[...] = srcs[0][...].astype(dst.dtype)
            else:
                nt = dst.shape[0] - pad
                for t in range(nt):
                    for k, src in enumerate(srcs):
                        dst[t, :, k * tile_w:(k + 1) * tile_w] = src[:, t * tile_w:(t + 1) * tile_w].astype(dst.dtype)
                for t in range(nt, nt + pad):
                    dst[t] = jnp.zeros(dst.shape[1:], dst.dtype)

        if tile_w is None and pad:
            @pl.when((step >= nblk) & (step < nblk + pad))
            def _():
                dst[...] = jnp.zeros(dst.shape, dst.dtype)


def _n_cast_srcs(metas):
    return sum(m[1] for m in metas)


def _sink_softmax(s, sink):
    m = jnp.maximum(jnp.max(s, axis=-1, keepdims=True), sink)
    e = jnp.exp(s - m)
    den = jnp.sum(e, axis=-1, keepdims=True) + jnp.exp(sink - m)
    return e * (1.0 / den)


def _attn_prompt_kernel(*refs, cast_metas):
    nc = _n_cast_srcs(cast_metas)
    sinks_ref, q_ref, kvp_ref, kvc_ref = refs[:4]
    o_ref = refs[4 + nc]
    rows = GROUP * BLOCK
    row = lax.broadcasted_iota(jnp.int32, (rows, BLOCK), 0) % BLOCK
    col = lax.broadcasted_iota(jnp.int32, (rows, BLOCK), 1)
    from_prev = col > row
    scale = HEAD_DIM ** -0.5
    nt_dims = (((1,), (1,)), ((), ()))

    def attend(first_block):
        _run_cast_jobs(pl.program_id(0) * pl.num_programs(1) + pl.program_id(1), cast_metas,
                       refs[4:4 + nc], refs[5 + nc:])
        for kh in range(N_KV_HEADS):
            ks = slice(kh * HEAD_DIM, (kh + 1) * HEAD_DIM)
            vs = slice(KV_WIDTH + kh * HEAD_DIM, KV_WIDTH + (kh + 1) * HEAD_DIM)
            if first_block:
                kmat, vmat = kvc_ref[:, ks].astype(BF16), kvc_ref[:, vs].astype(BF16)
            else:
                kmat = jnp.concatenate([kvp_ref[:, ks], kvc_ref[:, ks]], axis=0).astype(BF16)
                vmat = jnp.concatenate([kvp_ref[:, vs], kvc_ref[:, vs]], axis=0).astype(BF16)
            h0 = kh * GROUP
            qg = jnp.concatenate(
                [(q_ref[:, (h0 + g) * HEAD_DIM:(h0 + g + 1) * HEAD_DIM] * scale).astype(BF16) for g in range(GROUP)],
                axis=0)
            s2 = lax.dot_general(qg, kmat, nt_dims, preferred_element_type=F32)
            if first_block:
                s = jnp.where(from_prev, NEG_INF, s2)
            else:
                s = jnp.where(from_prev, s2[:, :BLOCK], s2[:, BLOCK:])
            sink = jnp.concatenate([jnp.full((BLOCK, BLOCK), sinks_ref[h0 + g], F32) for g in range(GROUP)], axis=0)
            m = jnp.maximum(jnp.max(s, axis=-1, keepdims=True), sink)
            e = jnp.exp(s - m)
            den = jnp.sum(e, axis=-1, keepdims=True) + jnp.exp(sink - m)
            p = e * (1.0 / den)
            if not first_block:
                p = jnp.concatenate([jnp.where(from_prev, p, 0.0), jnp.where(from_prev, 0.0, p)], axis=1)
            og = jnp.dot(p.astype(BF16), vmat, preferred_element_type=F32)
            for gp in range(GROUP // 2):
                c0 = (h0 + 2 * gp) * HEAD_DIM
                o_ref[:, c0:c0 + 2 * HEAD_DIM] = jnp.concatenate(
                    [og[2 * gp * BLOCK:(2 * gp + 1) * BLOCK], og[(2 * gp + 1) * BLOCK:(2 * gp + 2) * BLOCK]],
                    axis=1).astype(o_ref.dtype)

    @pl.when(pl.program_id(1) == 0)
    def _():
        attend(True)

    @pl.when(pl.program_id(1) > 0)
    def _():
        attend(False)


def _attn_prompt(p3, sinks, cast_jobs=()):
    assert WINDOW == BLOCK
    b, s, _ = p3.shape
    nb = s // BLOCK
    kv_blk = ATTN_WIDTH // (2 * KV_WIDTH)
    jobs = [_cast_job(*job, b * nb, lambda i, n: i * nb + n) for job in cast_jobs]
    return pl.pallas_call(
        functools.partial(_attn_prompt_kernel, cast_metas=tuple(j[0] for j in jobs)),
        out_shape=[jax.ShapeDtypeStruct((b, s, ATTN_WIDTH), BF16)] + [j[3] for j in jobs],
        grid=(b, nb),
        in_specs=[
            pl.BlockSpec(memory_space=pltpu.SMEM),
            pl.BlockSpec((None, BLOCK, ATTN_WIDTH), lambda i, n: (i, n, 0)),
            pl.BlockSpec((None, BLOCK, 2 * KV_WIDTH), lambda i, n: (i, jnp.maximum(n - 1, 0), kv_blk)),
            pl.BlockSpec((None, BLOCK, 2 * KV_WIDTH), lambda i, n: (i, n, kv_blk)),
        ] + [sp for j in jobs for sp in j[1]],
        out_specs=[pl.BlockSpec((None, BLOCK, ATTN_WIDTH), lambda i, n: (i, n, 0))] + [j[2] for j in jobs],
        compiler_params=_params(("arbitrary", "arbitrary"), 48),
        name="attn_prompt",
    )(sinks, p3, p3, p3, *[w for job in cast_jobs for w in job[0]])


def _attn_decode_kernel(sinks_ref, q_ref, kt_ref, vt_ref, kn_ref, vn_ref, o_ref, ktw_ref, vtw_ref):
    bb, _, w = kt_ref.shape
    last = lax.broadcasted_iota(jnp.int32, (KV_WIDTH, w), 1) == w - 1
    for old_ref, new_ref, win_ref in ((kt_ref, kn_ref, ktw_ref), (vt_ref, vn_ref, vtw_ref)):
        for b in range(bb):
            shifted = pltpu.roll(old_ref[b], w - 1, axis=1)
            newest = jnp.broadcast_to(new_ref[:, b:b + 1], (KV_WIDTH, w))
            win_ref[b] = jnp.where(last, newest, shifted)

    scale = HEAD_DIM ** -0.5
    row_kv = lax.broadcasted_iota(jnp.int32, (N_HEADS, KV_WIDTH), 0) // GROUP
    col_kv = lax.broadcasted_iota(jnp.int32, (N_HEADS, KV_WIDTH), 1) // HEAD_DIM
    q = q_ref[...] * scale
    q_bd = jnp.where(row_kv == col_kv, jnp.concatenate([q] * N_KV_HEADS, axis=-1), 0.0).astype(BF16)
    s = lax.dot_general(q_bd, ktw_ref[...].astype(BF16), (((2,), (1,)), ((0,), (0,))),
                        preferred_element_type=F32)
    p = _sink_softmax(s, sinks_ref[...])
    o_full = lax.dot_general(p.astype(BF16), vtw_ref[...].astype(BF16), (((2,), (2,)), ((0,), (0,))),
                             preferred_element_type=F32)
    head_kv = lax.broadcasted_iota(jnp.int32, (N_HEADS, HEAD_DIM), 0) // GROUP
    o = jnp.zeros(o_ref.shape, F32)
    for kh in range(N_KV_HEADS):
        o = o + jnp.where(head_kv == kh, o_full[:, :, kh * HEAD_DIM:(kh + 1) * HEAD_DIM], 0.0)
    o_ref[...] = o


def _attn_decode(q, kt_old, vt_old, k_new, v_new, sinks_col, *, bb=16):
    db = q.shape[0]
    w = kt_old.shape[2]
    nblk = db // bb
    win_spec = pl.BlockSpec((bb, KV_WIDTH, w), lambda i: (i, 0, 0))
    new_spec = pl.BlockSpec((None, KV_WIDTH, bb), lambda i: (i, 0, 0))
    kn, vn = (jnp.transpose(x.reshape(nblk, bb, KV_WIDTH), (0, 2, 1)) for x in (k_new, v_new))
    return pl.pallas_call(
        _attn_decode_kernel,
        out_shape=[jax.ShapeDtypeStruct((db, N_HEADS, HEAD_DIM), F32),
                   jax.ShapeDtypeStruct(kt_old.shape, F32), jax.ShapeDtypeStruct(vt_old.shape, F32)],
        grid=(nblk,),
        in_specs=[
            pl.BlockSpec((N_HEADS, 1), lambda i: (0, 0)),
            pl.BlockSpec((bb, N_HEADS, HEAD_DIM), lambda i: (i, 0, 0)),
            win_spec, win_spec, new_spec, new_spec,
        ],
        out_specs=[pl.BlockSpec((bb, N_HEADS, HEAD_DIM), lambda i: (i, 0, 0)), win_spec, win_spec],
        compiler_params=_params(("parallel",), 32),
        name="attn_decode",
    )(sinks_col, q, kt_old, vt_old, kn, vn)


def _pool_prompt_kernel(*refs, ts, cast_metas):
    ng = len(POOL_WINDOWS)
    nc = _n_cast_srcs(cast_metas)
    u_refs, halo_refs = refs[:ng], refs[ng:2 * ng]
    wp_ref, sc_ref = refs[2 * ng:2 * ng + 2]
    o_ref = refs[2 * ng + 2 + nc]
    ext_ref = refs[-1]
    _run_cast_jobs(pl.program_id(0) * pl.num_programs(1) + pl.program_id(1), cast_metas,
                   refs[2 * ng + 2:2 * ng + 2 + nc], refs[2 * ng + 3 + nc:-1])
    t = pl.program_id(1)
    halo = 16
    pos = t * ts + lax.broadcasted_iota(jnp.int32, (ts, 1), 0)
    gw = wp_ref.shape[-1]
    for g, w in enumerate(POOL_WINDOWS):
        x = u_refs[g][...]
        ext_ref[0:halo, :] = jnp.where(t > 0, halo_refs[g][...], 0.0)
        ext_ref[halo:halo + ts, :] = x
        acc = x
        for k in range(1, w):
            acc = acc + ext_ref[halo - k:halo - k + ts, :]
        cnt = jnp.minimum(pos + 1, w).astype(F32)
        d = acc / cnt - x
        y = jnp.dot(d.astype(BF16), wp_ref[g].astype(BF16), preferred_element_type=F32)
        o_ref[:, g * gw:(g + 1) * gw] = (y * sc_ref[:, g * gw:(g + 1) * gw]).astype(o_ref.dtype)


def _pool_prompt(p3, w_pool, scale, cast_jobs=(), *, ts=512):
    b, s, _ = p3.shape
    nt = s // ts
    jobs = [_cast_job(*job, b * nt, lambda i, t: i * nt + t) for job in cast_jobs]
    ng, gw, _ = w_pool.shape
    u_blk0 = (ATTN_WIDTH + 2 * KV_WIDTH) // gw
    halo = 16
    u_specs = [pl.BlockSpec((None, ts, gw), functools.partial(lambda i, t, g: (i, t, u_blk0 + g), g=g))
               for g in range(ng)]
    halo_specs = [
        pl.BlockSpec((None, halo, gw),
                     functools.partial(lambda i, t, g: (i, jnp.maximum(t * (ts // halo) - 1, 0), u_blk0 + g), g=g))
        for g in range(ng)]
    return pl.pallas_call(
        functools.partial(_pool_prompt_kernel, ts=ts, cast_metas=tuple(j[0] for j in jobs)),
        out_shape=[jax.ShapeDtypeStruct((b, s, ng * gw), BF16)] + [j[3] for j in jobs],
        grid=(b, nt),
        in_specs=u_specs + halo_specs + [
            pl.BlockSpec((ng, gw, gw), lambda i, t: (0, 0, 0)),
            pl.BlockSpec((1, ng * gw), lambda i, t: (0, 0)),
        ] + [sp for j in jobs for sp in j[1]],
        out_specs=[pl.BlockSpec((None, ts, ng * gw), lambda i, t: (i, t, 0))] + [j[2] for j in jobs],
        scratch_shapes=[pltpu.VMEM((halo + ts, gw), F32)],
        compiler_params=_params(("arbitrary", "arbitrary"), 48),
        name="pool_prompt",
    )(*([p3] * (2 * ng)), w_pool, scale, *[w for job in cast_jobs for w in job[0]])


def _pool_decode_kernel(st_ref, u_ref, wp_ref, sc_ref, o_ref, new_st_ref):
    gw = wp_ref.shape[-1]
    rows = st_ref.shape[0]
    for r in range(rows - 1):
        new_st_ref[r] = st_ref[r + 1]
    new_st_ref[rows - 1] = u_ref[...]
    for g, w in enumerate(POOL_WINDOWS):
        cs = slice(g * gw, (g + 1) * gw)
        x = u_ref[:, cs]
        acc = x
        for k in range(1, w):
            acc = acc + st_ref[rows - k, :, cs]
        d = acc / float(min(PAST_LEN + 1, w)) - x
        y = jnp.dot(d.astype(BF16), wp_ref[g].astype(BF16), preferred_element_type=F32)
        o_ref[:, cs] = (y * sc_ref[:, cs]).astype(o_ref.dtype)


def _pool_decode(state_t, u_new, w_pool, scale, *, bb=32):
    rows, db, pw = state_t.shape
    ng, gw, _ = w_pool.shape
    st_spec = pl.BlockSpec((rows, bb, pw), lambda i: (0, i, 0))
    return pl.pallas_call(
        _pool_decode_kernel,
        out_shape=[jax.ShapeDtypeStruct((db, pw), BF16), jax.ShapeDtypeStruct(state_t.shape, F32)],
        grid=(db // bb,),
        in_specs=[
            st_spec,
            pl.BlockSpec((bb, pw), lambda i: (i, 0)),
            pl.BlockSpec((ng, gw, gw), lambda i: (0, 0, 0)),
            pl.BlockSpec((1, pw), lambda i: (0, 0)),
        ],
        out_specs=[pl.BlockSpec((bb, pw), lambda i: (i, 0)), st_spec],
        compiler_params=_params(("parallel",), 40),
        name="pool_decode",
    )(state_t, u_new, w_pool, scale)


def _mix_kernel(*refs, cast_metas):
    nc = _n_cast_srcs(cast_metas)
    x_ref, a_ref, pl_ref, wa_ref, wp_ref = refs[:5]
    o_ref = refs[5 + nc]
    _run_cast_jobs(pl.program_id(0) * pl.num_programs(1) + pl.program_id(1), cast_metas,
                   refs[5:5 + nc], refs[6 + nc:])
    acc = jnp.dot(a_ref[...], wa_ref[...], preferred_element_type=F32)
    acc = acc + jnp.dot(pl_ref[...], wp_ref[...], preferred_element_type=F32)
    o_ref[...] = x_ref[...] + acc


def _mix(x, a, pooled, w_o, cast_jobs=(), *, tm, vmem_mib=48):
    m, d = x.shape
    aw = a.shape[1]
    pw = pooled.shape[1]
    nj, _, tn = w_o.shape
    assert aw == pw
    jobs = [_cast_job(*job, (m // tm) * nj, lambda i, j: i * nj + j) for job in cast_jobs]
    return pl.pallas_call(
        functools.partial(_mix_kernel, cast_metas=tuple(j[0] for j in jobs)),
        out_shape=[jax.ShapeDtypeStruct((m, d), F32)] + [j[3] for j in jobs],
        grid=(m // tm, nj),
        in_specs=[
            pl.BlockSpec((tm, tn), lambda i, j: (i, j)),
            pl.BlockSpec((tm, aw), lambda i, j: (i, 0)),
            pl.BlockSpec((tm, pw), lambda i, j: (i, 0)),
            pl.BlockSpec((None, aw, tn), lambda i, j: (j, 0, 0)),
            pl.BlockSpec((None, pw, tn), lambda i, j: (j, 1, 0)),
        ] + [sp for j in jobs for sp in j[1]],
        out_specs=[pl.BlockSpec((tm, tn), lambda i, j: (i, j))] + [j[2] for j in jobs],
        compiler_params=_params(("arbitrary", "arbitrary"), vmem_mib),
        name="mix",
    )(x, a, pooled, w_o, w_o, *[w for job in cast_jobs for w in job[0]])


def _ffn_tile(h, wgu, wd):
    tf = wd.shape[0]
    gate_up = jnp.dot(h, wgu, preferred_element_type=F32)
    gate, up = gate_up[:, :tf], gate_up[:, tf:]
    act = (gate * (1.0 / (1.0 + jnp.exp(-gate))) * up).astype(BF16)
    return jnp.dot(act, wd, preferred_element_type=F32)


def _ffn_kernel(x_hbm, g_ref, wgu_hbm, wd_hbm, gf_ref, y_hbm, h_ref, *, grid, specs, n_tiles, final_norm):
    del x_hbm
    per_step = specs[1].block_shape[0]
    tf = specs[2].block_shape[0] // per_step
    tail = n_tiles % per_step

    def step(acc_ref, wgu_ref, wd_ref):
        f = pl.program_id(1)

        def add_tiles(count):
            for k in range(count):
                acc_ref[...] += _ffn_tile(h_ref[...], wgu_ref[k], wd_ref[k * tf:(k + 1) * tf, :])

        @pl.when(f == 0)
        def _():
            _rmsnorm_rows(acc_ref, g_ref, h_ref)

        if tail:
            pl.when(f < grid[1] - 1)(lambda: add_tiles(per_step))
            pl.when(f == grid[1] - 1)(lambda: add_tiles(tail))
        else:
            add_tiles(per_step)

        if final_norm:
            @pl.when(f == grid[1] - 1)
            def _():
                _rmsnorm_rows(acc_ref, gf_ref, acc_ref)

    buffers = [
        pltpu.BufferedRef.input_output(specs[0], F32, 2),
        pltpu.BufferedRef.input(specs[1], BF16, 2),
        pltpu.BufferedRef.input(specs[2], BF16, 2),
    ]
    pipeline = pltpu.emit_pipeline(step, grid=grid, in_specs=specs, out_specs=[])
    pl.run_scoped(lambda bufs: pipeline(y_hbm, wgu_hbm, wd_hbm, allocations=bufs), buffers)


def _ffn(x, g, wgu, wd, g_final, *, tm, per_step, n_tiles, final_norm):
    m, d = x.shape
    nt, _, tf2 = wgu.shape
    grid = (m // tm, -(-n_tiles // per_step))
    assert nt >= grid[1] * per_step and wd.shape[0] == nt * tf2 // 2
    specs = [
        pl.BlockSpec((tm, d), lambda i, f: (i, 0)),
        pl.BlockSpec((per_step, d, tf2), lambda i, f: (f, 0, 0)),
        pl.BlockSpec((per_step * tf2 // 2, d), lambda i, f: (f, 0)),
    ]
    whole = pl.BlockSpec(memory_space=pltpu.VMEM)
    hbm = pl.BlockSpec(memory_space=pl.ANY)
    return pl.pallas_call(
        functools.partial(_ffn_kernel, grid=grid, specs=specs, n_tiles=n_tiles, final_norm=final_norm),
        out_shape=jax.ShapeDtypeStruct((m, d), F32),
        in_specs=[hbm, whole, hbm, hbm, whole],
        out_specs=hbm,
        input_output_aliases={0: 0},
        scratch_shapes=[pltpu.VMEM((tm, d), BF16)],
        compiler_params=pltpu.CompilerParams(vmem_limit_bytes=60 * MIB),
        name="ffn",
    )(x, g, wgu, wd, g_final)


def _decode_mask_is_trailing_window(w, t):
    pos_q = PAST_LEN + np.arange(t)
    pos_k = PAST_LEN - w + np.arange(w + t)
    mask = (pos_k[None, :] <= pos_q[:, None]) & (pos_q[:, None] - pos_k[None, :] < WINDOW)
    want = np.zeros_like(mask)
    want[:, t:] = True
    return bool((mask == want).all())


def kernel(x_prompt, x_sample, cache_k_win, cache_v_win, state_pool, g_mix, w_in, sinks, w_pool, pool_scale,
           w_o, g_ffn, w_gate, w_up, w_down, g_final):
    bsz, seq, d = x_prompt.shape
    db, t_new, _ = x_sample.shape
    depth = w_in.shape[0]
    w_rows = cache_k_win.shape[2]
    assert t_new == 1 and _decode_mask_is_trailing_window(w_rows, t_new)
    pw = min(WINDOW, seq)
    u0 = ATTN_WIDTH + 2 * KV_WIDTH

    xp = x_prompt.reshape(bsz * seq, d)
    xs = x_sample.reshape(db * t_new, d)
    gfin = g_final.reshape(1, d)
    outs = [[] for _ in range(6)]
    for l in range(depth):
        last = l == depth - 1
        w_in_l = w_in[l].astype(BF16)
        w_pool_l = w_pool[l]
        gm, gf = g_mix[l].reshape(1, d), g_ffn[l].reshape(1, d)
        scale = pool_scale[l].reshape(1, -1)

        p = _proj(xp, gm, w_in_l, tm=1024, w_buffers=2, vmem_mib=60)
        p3 = p.reshape(bsz, seq, -1)
        n_tiles = w_gate.shape[2] // FFN_TF
        ffn_pad = -n_tiles % max(FFN_TILES_PER_STEP)
        a, wgu_l = _attn_prompt(p3, sinks[l], [((w_gate[l], w_up[l]), FFN_TF, ffn_pad)])
        pooled, w_o_l = _pool_prompt(p3, w_pool_l, scale, [((w_o[l],), MIX_TN, 0)])
        x1, wd_l = _mix(xp, a.reshape(bsz * seq, -1), pooled.reshape(bsz * seq, -1), w_o_l,
                        [((w_down[l],), None, ffn_pad)], tm=1024, vmem_mib=56)
        xp = _ffn(x1, gf, wgu_l, wd_l, gfin, tm=1024, per_step=FFN_TILES_PER_STEP[0], n_tiles=n_tiles,
                  final_norm=last)
        outs[0].append(p3[:, seq - pw:, ATTN_WIDTH:ATTN_WIDTH + KV_WIDTH].reshape(bsz, pw, N_KV_HEADS, HEAD_DIM))
        outs[1].append(p3[:, seq - pw:, ATTN_WIDTH + KV_WIDTH:u0].reshape(bsz, pw, N_KV_HEADS, HEAD_DIM))
        outs[2].append(p3[:, seq - POOL_STATE_ROWS:, u0:])

        ps = _proj(xs, gm, w_in_l, tm=db)
        u_new = ps[:, u0:]
        kt_old = jnp.transpose(cache_k_win[l], (0, 2, 3, 1)).reshape(db, KV_WIDTH, w_rows)
        vt_old = jnp.transpose(cache_v_win[l], (0, 2, 3, 1)).reshape(db, KV_WIDTH, w_rows)
        a_s, kt_win, vt_win = _attn_decode(ps[:, :ATTN_WIDTH].reshape(db, N_HEADS, HEAD_DIM), kt_old, vt_old,
                                           ps[:, ATTN_WIDTH:ATTN_WIDTH + KV_WIDTH], ps[:, ATTN_WIDTH + KV_WIDTH:u0],
                                           sinks[l].reshape(N_HEADS, 1))
        pooled_s, state_t = _pool_decode(jnp.transpose(state_pool[l], (1, 0, 2)), u_new, w_pool_l, scale)
        x1s, = _mix(xs, a_s.reshape(db, ATTN_WIDTH).astype(BF16), pooled_s, w_o_l, tm=db)
        xs = _ffn(x1s, gf, wgu_l, wd_l, gfin, tm=db, per_step=FFN_TILES_PER_STEP[1], n_tiles=n_tiles,
                  final_norm=last)
        outs[3].append(jnp.transpose(kt_win.reshape(db, N_KV_HEADS, HEAD_DIM, w_rows), (0, 3, 1, 2)))
        outs[4].append(jnp.transpose(vt_win.reshape(db, N_KV_HEADS, HEAD_DIM, w_rows), (0, 3, 1, 2)))
        outs[5].append(jnp.transpose(state_t, (1, 0, 2)))

    return (xp.reshape(bsz, seq, d), xs.reshape(db, t_new, d)) + tuple(jnp.stack(o) for o in outs)
```

```python
import functools

import jax
import jax.numpy as jnp
import numpy as np
from jax import lax
from jax.experimental import pallas as pl
from jax.experimental.pallas import tpu as pltpu

HEAD_DIM = 64
N_KV_HEADS = 4
GROUP = 8
N_HEADS = N_KV_HEADS * GROUP
ATTN_WIDTH = N_HEADS * HEAD_DIM
KV_WIDTH = N_KV_HEADS * HEAD_DIM
WINDOW = 128
BLOCK = 128
PAST_LEN = 8192
POOL_WINDOWS = (2, 4, 8, 16)
POOL_STATE_ROWS = max(POOL_WINDOWS) - 1
NORM_EPS = 1e-5
NEG_INF = -1e30

PROJ_TN = 512
MIX_TN = 512
FFN_TF = 256
FFN_TILES_PER_STEP = (1, 2)

BF16 = jnp.bfloat16
F32 = jnp.float32
MIB = 1024 * 1024


def _rmsnorm(xf, g):
    return xf * lax.rsqrt(jnp.mean(xf * xf, axis=-1, keepdims=True) + NORM_EPS) * g


def _params(semantics, vmem_mib):
    return pltpu.CompilerParams(dimension_semantics=semantics, vmem_limit_bytes=vmem_mib * MIB)


ROW_CHUNK = 512


def _rmsnorm_rows(src_ref, g_ref, dst_ref, chunk=ROW_CHUNK):
    rows = src_ref.shape[0]
    chunk = min(chunk, rows)
    for r0 in range(0, rows, chunk):
        dst_ref[r0:r0 + chunk, :] = _rmsnorm(src_ref[r0:r0 + chunk, :], g_ref[...]).astype(dst_ref.dtype)


def _proj_kernel(x_hbm, g_ref, w_hbm, p_hbm, h_ref, *, grid, specs):
    def step(x_ref, w_ref, p_ref):
        rows = x_ref.shape[0]
        chunk = min(ROW_CHUNK, rows)

        @pl.when(pl.program_id(1) == 0)
        def _():
            for r0 in range(0, rows, chunk):
                rs = slice(r0, r0 + chunk)
                h_ref[rs, :] = _rmsnorm(x_ref[rs, :], g_ref[...]).astype(h_ref.dtype)
                p_ref[rs, :] = jnp.dot(h_ref[rs, :], w_ref[...], preferred_element_type=F32)

        @pl.when(pl.program_id(1) > 0)
        def _():
            p_ref[...] = jnp.dot(h_ref[...], w_ref[...], preferred_element_type=F32)

    pltpu.emit_pipeline(step, grid=grid, in_specs=specs[:2], out_specs=specs[2:])(x_hbm, w_hbm, p_hbm)


def _early(n_row_tiles):
    return pl.Buffered(2, use_lookahead=True) if n_row_tiles > 1 else None


def _proj(x, g, w, *, tm, tn=PROJ_TN, w_buffers=2, vmem_mib=48):
    m, d = x.shape
    n = w.shape[1]
    grid = (m // tm, n // tn)
    specs = [
        pl.BlockSpec((tm, d), lambda i, j: (i, 0), pipeline_mode=_early(grid[0])),
        pl.BlockSpec((d, tn), lambda i, j: (0, j), pipeline_mode=pl.Buffered(w_buffers)),
        pl.BlockSpec((tm, tn), lambda i, j: (i, j)),
    ]
    whole = pl.BlockSpec(memory_space=pltpu.VMEM)
    hbm = pl.BlockSpec(memory_space=pl.ANY)
    return pl.pallas_call(
        functools.partial(_proj_kernel, grid=grid, specs=specs),
        out_shape=jax.ShapeDtypeStruct((m, n), F32),
        in_specs=[hbm, whole, hbm],
        out_specs=hbm,
        scratch_shapes=[pltpu.VMEM((tm, d), BF16)],
        compiler_params=pltpu.CompilerParams(vmem_limit_bytes=vmem_mib * MIB),
        name="proj",
    )(x, g, w)


def _proj_cast_kernel(x_ref, g_ref, w_ref, p_ref, wb_ref, h_ref):
    @pl.when(pl.program_id(0) == 0)
    def _():
        _rmsnorm_rows(x_ref, g_ref, h_ref)

    wb = w_ref[...].astype(BF16)
    wb_ref[...] = wb
    p_ref[...] = jnp.dot(h_ref[...], wb, preferred_element_type=F32)


def _proj_cast(x, g, w, *, tn=PROJ_TN):
    m, d = x.shape
    n = w.shape[1]
    return pl.pallas_call(
        _proj_cast_kernel,
        out_shape=[jax.ShapeDtypeStruct((m, n), F32), jax.ShapeDtypeStruct((d, n), BF16)],
        grid=(n // tn,),
        in_specs=[
            pl.BlockSpec((m, d), lambda j: (0, 0)),
            pl.BlockSpec((1, d), lambda j: (0, 0)),
            pl.BlockSpec((d, tn), lambda j: (0, j)),
        ],
        out_specs=[pl.BlockSpec((m, tn), lambda j: (0, j)), pl.BlockSpec((d, tn), lambda j: (0, j))],
        scratch_shapes=[pltpu.VMEM((m, d), BF16)],
        compiler_params=_params(("arbitrary",), 48),
        name="proj_cast",
    )(x, g, w)


def _cast_blocks(rows, steps):
    for nblk in range(min(steps, rows // 16), 0, -1):
        if rows % nblk == 0 and (rows // nblk) % 16 == 0:
            return nblk
    raise ValueError(f"no bf16-tileable row split of {rows} rows")


def _cast_job(srcs, tile_w, pad, steps, step_of):
    rows, cols = srcs[0].shape
    nblk = _cast_blocks(rows, steps)
    rb = rows // nblk
    in_specs = [pl.BlockSpec((rb, cols), lambda *idx: (jnp.minimum(step_of(*idx), nblk - 1), 0)) for _ in srcs]
    if tile_w is None:
        assert len(srcs) == 1 and nblk + pad <= steps
        out_spec = pl.BlockSpec((rb, cols), lambda *idx: (jnp.minimum(step_of(*idx), nblk + pad - 1), 0))
        return (nblk, 1, None, pad, False), in_specs, out_spec, jax.ShapeDtypeStruct((rows + pad * rb, cols), BF16)
    nt, wide = cols // tile_w, tile_w * len(srcs)
    out_spec = pl.BlockSpec((nt + pad, rb, wide), lambda *idx: (0, jnp.minimum(step_of(*idx), nblk - 1), 0))
    meta = (nblk, len(srcs), tile_w, pad, nblk == steps)
    return meta, in_specs, out_spec, jax.ShapeDtypeStruct((nt + pad, rows, wide), BF16)


def _run_cast_jobs(step, metas, src_refs, dst_refs):
    src_refs = list(src_refs)
    for (nblk, nsrc, tile_w, pad, every_step), dst in zip(metas, dst_refs):
        srcs, src_refs = src_refs[:nsrc], src_refs[nsrc:]

        @(lambda body: body()) if every_step else pl.when(step < nblk)
        def _():
            if tile_w is None:
                dst[...] = srcs[0][...].astype(dst.dtype)
            else:
                nt = dst.shape[0] - pad
                for t in range(nt):
                    for k, src in enumerate(srcs):
                        dst[t, :, k * tile_w:(k + 1) * tile_w] = src[:, t * tile_w:(t + 1) * tile_w].astype(dst.dtype)
                for t in range(nt, nt + pad):
                    dst[t] = jnp.zeros(dst.shape[1:], dst.dtype)

        if tile_w is None and pad:
            @pl.when((step >= nblk) & (step < nblk + pad))
            def _():
                dst[...] = jnp.zeros(dst.shape, dst.dtype)


def _n_cast_srcs(metas):
    return sum(m[1] for m in metas)


def _sink_softmax(s, sink):
    m = jnp.maximum(jnp.max(s, axis=-1, keepdims=True), sink)
    e = jnp.exp(s - m)
    den = jnp.sum(e, axis=-1, keepdims=True) + jnp.exp(sink - m)
    return e * (1.0 / den)


def _attn_prompt_kernel(*refs, cast_metas):
    nc = _n_cast_srcs(cast_metas)
    sinks_ref, q_ref, kvp_ref, kvc_ref = refs[:4]
    o_ref = refs[4 + nc]
    rows = GROUP * BLOCK
    row = lax.broadcasted_iota(jnp.int32, (rows, BLOCK), 0) % BLOCK
    col = lax.broadcasted_iota(jnp.int32, (rows, BLOCK), 1)
    from_prev = col > row
    scale = HEAD_DIM ** -0.5
    nt_dims = (((1,), (1,)), ((), ()))

    def attend(first_block):
        _run_cast_jobs(pl.program_id(0) * pl.num_programs(1) + pl.program_id(1), cast_metas,
                       refs[4:4 + nc], refs[5 + nc:])
        for kh in range(N_KV_HEADS):
            ks = slice(kh * HEAD_DIM, (kh + 1) * HEAD_DIM)
            vs = slice(KV_WIDTH + kh * HEAD_DIM, KV_WIDTH + (kh + 1) * HEAD_DIM)
            if first_block:
                kmat, vmat = kvc_ref[:, ks].astype(BF16), kvc_ref[:, vs].astype(BF16)
            else:
                kmat = jnp.concatenate([kvp_ref[:, ks], kvc_ref[:, ks]], axis=0).astype(BF16)
                vmat = jnp.concatenate([kvp_ref[:, vs], kvc_ref[:, vs]], axis=0).astype(BF16)
            h0 = kh * GROUP
            qg = jnp.concatenate(
                [(q_ref[:, (h0 + g) * HEAD_DIM:(h0 + g + 1) * HEAD_DIM] * scale).astype(BF16) for g in range(GROUP)],
                axis=0)
            s2 = lax.dot_general(qg, kmat, nt_dims, preferred_element_type=F32)
            if first_block:
                s = jnp.where(from_prev, NEG_INF, s2)
            else:
                s = jnp.where(from_prev, s2[:, :BLOCK], s2[:, BLOCK:])
            sink = jnp.concatenate([jnp.full((BLOCK, BLOCK), sinks_ref[h0 + g], F32) for g in range(GROUP)], axis=0)
            m = jnp.maximum(jnp.max(s, axis=-1, keepdims=True), sink)
            e = jnp.exp(s - m)
            den = jnp.sum(e, axis=-1, keepdims=True) + jnp.exp(sink - m)
            p = e * (1.0 / den)
            if not first_block:
                p = jnp.concatenate([jnp.where(from_prev, p, 0.0), jnp.where(from_prev, 0.0, p)], axis=1)
            og = jnp.dot(p.astype(BF16), vmat, preferred_element_type=F32)
            for gp in range(GROUP // 2):
                c0 = (h0 + 2 * gp) * HEAD_DIM
                o_ref[:, c0:c0 + 2 * HEAD_DIM] = jnp.concatenate(
                    [og[2 * gp * BLOCK:(2 * gp + 1) * BLOCK], og[(2 * gp + 1) * BLOCK:(2 * gp + 2) * BLOCK]],
                    axis=1).astype(o_ref.dtype)

    @pl.when(pl.program_id(1) == 0)
    def _():
        attend(True)

    @pl.when(pl.program_id(1) > 0)
    def _():
        attend(False)


def _attn_prompt(p3, sinks, cast_jobs=()):
    assert WINDOW == BLOCK
    b, s, _ = p3.shape
    nb = s // BLOCK
    kv_blk = ATTN_WIDTH // (2 * KV_WIDTH)
    jobs = [_cast_job(*job, b * nb, lambda i, n: i * nb + n) for job in cast_jobs]
    return pl.pallas_call(
        functools.partial(_attn_prompt_kernel, cast_metas=tuple(j[0] for j in jobs)),
        out_shape=[jax.ShapeDtypeStruct((b, s, ATTN_WIDTH), BF16)] + [j[3] for j in jobs],
        grid=(b, nb),
        in_specs=[
            pl.BlockSpec(memory_space=pltpu.SMEM),
            pl.BlockSpec((None, BLOCK, ATTN_WIDTH), lambda i, n: (i, n, 0)),
            pl.BlockSpec((None, BLOCK, 2 * KV_WIDTH), lambda i, n: (i, jnp.maximum(n - 1, 0), kv_blk)),
            pl.BlockSpec((None, BLOCK, 2 * KV_WIDTH), lambda i, n: (i, n, kv_blk)),
        ] + [sp for j in jobs for sp in j[1]],
        out_specs=[pl.BlockSpec((None, BLOCK, ATTN_WIDTH), lambda i, n: (i, n, 0))] + [j[2] for j in jobs],
        compiler_params=_params(("arbitrary", "arbitrary"), 48),
        name="attn_prompt",
    )(sinks, p3, p3, p3, *[w for job in cast_jobs for w in job[0]])


def _attn_decode_kernel(sinks_ref, q_ref, kt_ref, vt_ref, kn_ref, vn_ref, o_ref, ktw_ref, vtw_ref):
    bb, _, w = kt_ref.shape
    last = lax.broadcasted_iota(jnp.int32, (KV_WIDTH, w), 1) == w - 1
    for old_ref, new_ref, win_ref in ((kt_ref, kn_ref, ktw_ref), (vt_ref, vn_ref, vtw_ref)):
        for b in range(bb):
            shifted = pltpu.roll(old_ref[b], w - 1, axis=1)
            newest = jnp.broadcast_to(new_ref[:, b:b + 1], (KV_WIDTH, w))
            win_ref[b] = jnp.where(last, newest, shifted)

    scale = HEAD_DIM ** -0.5
    row_kv = lax.broadcasted_iota(jnp.int32, (N_HEADS, KV_WIDTH), 0) // GROUP
    col_kv = lax.broadcasted_iota(jnp.int32, (N_HEADS, KV_WIDTH), 1) // HEAD_DIM
    q = q_ref[...] * scale
    q_bd = jnp.where(row_kv == col_kv, jnp.concatenate([q] * N_KV_HEADS, axis=-1), 0.0).astype(BF16)
    s = lax.dot_general(q_bd, ktw_ref[...].astype(BF16), (((2,), (1,)), ((0,), (0,))),
                        preferred_element_type=F32)
    p = _sink_softmax(s, sinks_ref[...])
    o_full = lax.dot_general(p.astype(BF16), vtw_ref[...].astype(BF16), (((2,), (2,)), ((0,), (0,))),
                             preferred_element_type=F32)
    head_kv = lax.broadcasted_iota(jnp.int32, (N_HEADS, HEAD_DIM), 0) // GROUP
    o = jnp.zeros(o_ref.shape, F32)
    for kh in range(N_KV_HEADS):
        o = o + jnp.where(head_kv == kh, o_full[:, :, kh * HEAD_DIM:(kh + 1) * HEAD_DIM], 0.0)
    o_ref[...] = o


def _attn_decode(q, kt_old, vt_old, k_new, v_new, sinks_col, *, bb=16):
    db = q.shape[0]
    w = kt_old.shape[2]
    nblk = db // bb
    win_spec = pl.BlockSpec((bb, KV_WIDTH, w), lambda i: (i, 0, 0))
    new_spec = pl.BlockSpec((None, KV_WIDTH, bb), lambda i: (i, 0, 0))
    kn, vn = (jnp.transpose(x.reshape(nblk, bb, KV_WIDTH), (0, 2, 1)) for x in (k_new, v_new))
    return pl.pallas_call(
        _attn_decode_kernel,
        out_shape=[jax.ShapeDtypeStruct((db, N_HEADS, HEAD_DIM), F32),
                   jax.ShapeDtypeStruct(kt_old.shape, F32), jax.ShapeDtypeStruct(vt_old.shape, F32)],
        grid=(nblk,),
        in_specs=[
            pl.BlockSpec((N_HEADS, 1), lambda i: (0, 0)),
            pl.BlockSpec((bb, N_HEADS, HEAD_DIM), lambda i: (i, 0, 0)),
            win_spec, win_spec, new_spec, new_spec,
        ],
        out_specs=[pl.BlockSpec((bb, N_HEADS, HEAD_DIM), lambda i: (i, 0, 0)), win_spec, win_spec],
        compiler_params=_params(("parallel",), 32),
        name="attn_decode",
    )(sinks_col, q, kt_old, vt_old, kn, vn)


def _pool_prompt_kernel(*refs, ts, cast_metas):
    ng = len(POOL_WINDOWS)
    nc = _n_cast_srcs(cast_metas)
    u_refs, halo_refs = refs[:ng], refs[ng:2 * ng]
    wp_ref, sc_ref = refs[2 * ng:2 * ng + 2]
    o_ref = refs[2 * ng + 2 + nc]
    ext_ref = refs[-1]
    _run_cast_jobs(pl.program_id(0) * pl.num_programs(1) + pl.program_id(1), cast_metas,
                   refs[2 * ng + 2:2 * ng + 2 + nc], refs[2 * ng + 3 + nc:-1])
    t = pl.program_id(1)
    halo = 16
    pos = t * ts + lax.broadcasted_iota(jnp.int32, (ts, 1), 0)
    gw = wp_ref.shape[-1]
    for g, w in enumerate(POOL_WINDOWS):
        x = u_refs[g][...]
        ext_ref[0:halo, :] = jnp.where(t > 0, halo_refs[g][...], 0.0)
        ext_ref[halo:halo + ts, :] = x
        acc = x
        for k in range(1, w):
            acc = acc + ext_ref[halo - k:halo - k + ts, :]
        cnt = jnp.minimum(pos + 1, w).astype(F32)
        d = acc / cnt - x
        y = jnp.dot(d.astype(BF16), wp_ref[g].astype(BF16), preferred_element_type=F32)
        o_ref[:, g * gw:(g + 1) * gw] = (y * sc_ref[:, g * gw:(g + 1) * gw]).astype(o_ref.dtype)


def _pool_prompt(p3, w_pool, scale, cast_jobs=(), *, ts=512):
    b, s, _ = p3.shape
    nt = s // ts
    jobs = [_cast_job(*job, b * nt, lambda i, t: i * nt + t) for job in cast_jobs]
    ng, gw, _ = w_pool.shape
    u_blk0 = (ATTN_WIDTH + 2 * KV_WIDTH) // gw
    halo = 16
    u_specs = [pl.BlockSpec((None, ts, gw), functools.partial(lambda i, t, g: (i, t, u_blk0 + g), g=g))
               for g in range(ng)]
    halo_specs = [
        pl.BlockSpec((None, halo, gw),
                     functools.partial(lambda i, t, g: (i, jnp.maximum(t * (ts // halo) - 1, 0), u_blk0 + g), g=g))
        for g in range(ng)]
    return pl.pallas_call(
        functools.partial(_pool_prompt_kernel, ts=ts, cast_metas=tuple(j[0] for j in jobs)),
        out_shape=[jax.ShapeDtypeStruct((b, s, ng * gw), BF16)] + [j[3] for j in jobs],
        grid=(b, nt),
        in_specs=u_specs + halo_specs + [
            pl.BlockSpec((ng, gw, gw), lambda i, t: (0, 0, 0)),
            pl.BlockSpec((1, ng * gw), lambda i, t: (0, 0)),
        ] + [sp for j in jobs for sp in j[1]],
        out_specs=[pl.BlockSpec((None, ts, ng * gw), lambda i, t: (i, t, 0))] + [j[2] for j in jobs],
        scratch_shapes=[pltpu.VMEM((halo + ts, gw), F32)],
        compiler_params=_params(("arbitrary", "arbitrary"), 48),
        name="pool_prompt",
    )(*([p3] * (2 * ng)), w_pool, scale, *[w for job in cast_jobs for w in job[0]])


def _pool_decode_kernel(st_ref, u_ref, wp_ref, sc_ref, o_ref, new_st_ref):
    gw = wp_ref.shape[-1]
    rows = st_ref.shape[0]
    for r in range(rows - 1):
        new_st_ref[r] = st_ref[r + 1]
    new_st_ref[rows - 1] = u_ref[...]
    for g, w in enumerate(POOL_WINDOWS):
        cs = slice(g * gw, (g + 1) * gw)
        x = u_ref[:, cs]
        acc = x
        for k in range(1, w):
            acc = acc + st_ref[rows - k, :, cs]
        d = acc / float(min(PAST_LEN + 1, w)) - x
        y = jnp.dot(d.astype(BF16), wp_ref[g].astype(BF16), preferred_element_type=F32)
        o_ref[:, cs] = (y * sc_ref[:, cs]).astype(o_ref.dtype)


def _pool_decode(state_t, u_new, w_pool, scale, *, bb=32):
    rows, db, pw = state_t.shape
    ng, gw, _ = w_pool.shape
    st_spec = pl.BlockSpec((rows, bb, pw), lambda i: (0, i, 0))
    return pl.pallas_call(
        _pool_decode_kernel,
        out_shape=[jax.ShapeDtypeStruct((db, pw), BF16), jax.ShapeDtypeStruct(state_t.shape, F32)],
        grid=(db // bb,),
        in_specs=[
            st_spec,
            pl.BlockSpec((bb, pw), lambda i: (i, 0)),
            pl.BlockSpec((ng, gw, gw), lambda i: (0, 0, 0)),
            pl.BlockSpec((1, pw), lambda i: (0, 0)),
        ],
        out_specs=[pl.BlockSpec((bb, pw), lambda i: (i, 0)), st_spec],
        compiler_params=_params(("parallel",), 40),
        name="pool_decode",
    )(state_t, u_new, w_pool, scale)


def _mix_kernel(*refs, cast_metas):
    nc = _n_cast_srcs(cast_metas)
    x_ref, a_ref, pl_ref, wa_ref, wp_ref = refs[:5]
    o_ref = refs[5 + nc]
    _run_cast_jobs(pl.program_id(0) * pl.num_programs(1) + pl.program_id(1), cast_metas,
                   refs[5:5 + nc], refs[6 + nc:])
    acc = jnp.dot(a_ref[...], wa_ref[...], preferred_element_type=F32)
    acc = acc + jnp.dot(pl_ref[...], wp_ref[...], preferred_element_type=F32)
    o_ref[...] = x_ref[...] + acc


def _mix(x, a, pooled, w_o, cast_jobs=(), *, tm, vmem_mib=48):
    m, d = x.shape
    aw = a.shape[1]
    pw = pooled.shape[1]
    nj, _, tn = w_o.shape
    assert aw == pw
    jobs = [_cast_job(*job, (m // tm) * nj, lambda i, j: i * nj + j) for job in cast_jobs]
    return pl.pallas_call(
        functools.partial(_mix_kernel, cast_metas=tuple(j[0] for j in jobs)),
        out_shape=[jax.ShapeDtypeStruct((m, d), F32)] + [j[3] for j in jobs],
        grid=(m // tm, nj),
        in_specs=[
            pl.BlockSpec((tm, tn), lambda i, j: (i, j)),
            pl.BlockSpec((tm, aw), lambda i, j: (i, 0)),
            pl.BlockSpec((tm, pw), lambda i, j: (i, 0)),
            pl.BlockSpec((None, aw, tn), lambda i, j: (j, 0, 0)),
            pl.BlockSpec((None, pw, tn), lambda i, j: (j, 1, 0)),
        ] + [sp for j in jobs for sp in j[1]],
        out_specs=[pl.BlockSpec((tm, tn), lambda i, j: (i, j))] + [j[2] for j in jobs],
        compiler_params=_params(("arbitrary", "arbitrary"), vmem_mib),
        name="mix",
    )(x, a, pooled, w_o, w_o, *[w for job in cast_jobs for w in job[0]])


def _ffn_tile(h, wgu, wd):
    tf = wd.shape[0]
    gate_up = jnp.dot(h, wgu, preferred_element_type=F32)
    gate, up = gate_up[:, :tf], gate_up[:, tf:]
    act = (gate * (1.0 / (1.0 + jnp.exp(-gate))) * up).astype(BF16)
    return jnp.dot(act, wd, preferred_element_type=F32)


def _ffn_kernel(x_hbm, g_ref, wgu_hbm, wd_hbm, gf_ref, y_hbm, h_ref, *, grid, specs, n_tiles, final_norm):
    del x_hbm
    per_step = specs[1].block_shape[0]
    tf = specs[2].block_shape[0] // per_step
    tail = n_tiles % per_step

    assert grid[1] >= 2

    def step(acc_ref, wgu_ref, wd_ref):
        f = pl.program_id(1)
        rows = acc_ref.shape[0]
        chunk = min(ROW_CHUNK, rows)

        def add_tiles(count, rs=slice(None)):
            for k in range(count):
                acc_ref[rs, :] += _ffn_tile(h_ref[rs, :], wgu_ref[k], wd_ref[k * tf:(k + 1) * tf, :])

        @pl.when(f == 0)
        def _():
            for r0 in range(0, rows, chunk):
                rs = slice(r0, r0 + chunk)
                h_ref[rs, :] = _rmsnorm(acc_ref[rs, :], g_ref[...]).astype(h_ref.dtype)
                add_tiles(per_step, rs)

        @pl.when((f > 0) & (f < grid[1] - 1))
        def _():
            add_tiles(per_step)

        @pl.when(f == grid[1] - 1)
        def _():
            for r0 in range(0, rows, chunk):
                rs = slice(r0, r0 + chunk)
                add_tiles(tail or per_step, rs)
                if final_norm:
                    acc_ref[rs, :] = _rmsnorm(acc_ref[rs, :], gf_ref[...])

    buffers = [
        pltpu.BufferedRef.input_output(specs[0], F32, 2),
        pltpu.BufferedRef.input(specs[1], BF16, 2),
        pltpu.BufferedRef.input(specs[2], BF16, 2),
    ]
    pipeline = pltpu.emit_pipeline(step, grid=grid, in_specs=specs, out_specs=[])
    pl.run_scoped(lambda bufs: pipeline(y_hbm, wgu_hbm, wd_hbm, allocations=bufs), buffers)


def _ffn(x, g, wgu, wd, g_final, *, tm, per_step, n_tiles, final_norm):
    m, d = x.shape
    nt, _, tf2 = wgu.shape
    grid = (m // tm, -(-n_tiles // per_step))
    assert nt >= grid[1] * per_step and wd.shape[0] == nt * tf2 // 2
    specs = [
        pl.BlockSpec((tm, d), lambda i, f: (i, 0)),
        pl.BlockSpec((per_step, d, tf2), lambda i, f: (f, 0, 0)),
        pl.BlockSpec((per_step * tf2 // 2, d), lambda i, f: (f, 0)),
    ]
    whole = pl.BlockSpec(memory_space=pltpu.VMEM)
    hbm = pl.BlockSpec(memory_space=pl.ANY)
    return pl.pallas_call(
        functools.partial(_ffn_kernel, grid=grid, specs=specs, n_tiles=n_tiles, final_norm=final_norm),
        out_shape=jax.ShapeDtypeStruct((m, d), F32),
        in_specs=[hbm, whole, hbm, hbm, whole],
        out_specs=hbm,
        input_output_aliases={0: 0},
        scratch_shapes=[pltpu.VMEM((tm, d), BF16)],
        compiler_params=pltpu.CompilerParams(vmem_limit_bytes=60 * MIB),
        name="ffn",
    )(x, g, wgu, wd, g_final)


def _decode_mask_is_trailing_window(w, t):
    pos_q = PAST_LEN + np.arange(t)
    pos_k = PAST_LEN - w + np.arange(w + t)
    mask = (pos_k[None, :] <= pos_q[:, None]) & (pos_q[:, None] - pos_k[None, :] < WINDOW)
    want = np.zeros_like(mask)
    want[:, t:] = True
    return bool((mask == want).all())


def kernel(x_prompt, x_sample, cache_k_win, cache_v_win, state_pool, g_mix, w_in, sinks, w_pool, pool_scale,
           w_o, g_ffn, w_gate, w_up, w_down, g_final):
    bsz, seq, d = x_prompt.shape
    db, t_new, _ = x_sample.shape
    depth = w_in.shape[0]
    w_rows = cache_k_win.shape[2]
    assert t_new == 1 and _decode_mask_is_trailing_window(w_rows, t_new)
    pw = min(WINDOW, seq)
    u0 = ATTN_WIDTH + 2 * KV_WIDTH

    xp = x_prompt.reshape(bsz * seq, d)
    xs = x_sample.reshape(db * t_new, d)
    gfin = g_final.reshape(1, d)
    outs = [[] for _ in range(6)]
    for l in range(depth):
        last = l == depth - 1
        w_pool_l = w_pool[l]
        gm, gf = g_mix[l].reshape(1, d), g_ffn[l].reshape(1, d)
        scale = pool_scale[l].reshape(1, -1)

        ps, w_in_l = _proj_cast(xs, gm, w_in[l])

        p = _proj(xp, gm, w_in_l, tm=1024, w_buffers=2, vmem_mib=60)
        p3 = p.reshape(bsz, seq, -1)
        n_tiles = w_gate.shape[2] // FFN_TF
        ffn_pad = -n_tiles % max(FFN_TILES_PER_STEP)
        a, wgu_l = _attn_prompt(p3, sinks[l], [((w_gate[l], w_up[l]), FFN_TF, ffn_pad)])
        pooled, w_o_l = _pool_prompt(p3, w_pool_l, scale, [((w_o[l],), MIX_TN, 0)])
        x1, wd_l = _mix(xp, a.reshape(bsz * seq, -1), pooled.reshape(bsz * seq, -1), w_o_l,
                        [((w_down[l],), None, ffn_pad)], tm=1024, vmem_mib=56)
        xp = _ffn(x1, gf, wgu_l, wd_l, gfin, tm=1024, per_step=FFN_TILES_PER_STEP[0], n_tiles=n_tiles,
                  final_norm=last)
        outs[0].append(p3[:, seq - pw:, ATTN_WIDTH:ATTN_WIDTH + KV_WIDTH].reshape(bsz, pw, N_KV_HEADS, HEAD_DIM))
        outs[1].append(p3[:, seq - pw:, ATTN_WIDTH + KV_WIDTH:u0].reshape(bsz, pw, N_KV_HEADS, HEAD_DIM))
        outs[2].append(p3[:, seq - POOL_STATE_ROWS:, u0:])

        u_new = ps[:, u0:]
        kt_old = jnp.transpose(cache_k_win[l], (0, 2, 3, 1)).reshape(db, KV_WIDTH, w_rows)
        vt_old = jnp.transpose(cache_v_win[l], (0, 2, 3, 1)).reshape(db, KV_WIDTH, w_rows)
        a_s, kt_win, vt_win = _attn_decode(ps[:, :ATTN_WIDTH].reshape(db, N_HEADS, HEAD_DIM), kt_old, vt_old,
                                           ps[:, ATTN_WIDTH:ATTN_WIDTH + KV_WIDTH], ps[:, ATTN_WIDTH + KV_WIDTH:u0],
                                           sinks[l].reshape(N_HEADS, 1))
        pooled_s, state_t = _pool_decode(jnp.transpose(state_pool[l], (1, 0, 2)), u_new, w_pool_l, scale)
        x1s, = _mix(xs, a_s.reshape(db, ATTN_WIDTH).astype(BF16), pooled_s, w_o_l, tm=db)
        xs = _ffn(x1s, gf, wgu_l, wd_l, gfin, tm=db, per_step=FFN_TILES_PER_STEP[1], n_tiles=n_tiles,
                  final_norm=last)
        outs[3].append(jnp.transpose(kt_win.reshape(db, N_KV_HEADS, HEAD_DIM, w_rows), (0, 3, 1, 2)))
        outs[4].append(jnp.transpose(vt_win.reshape(db, N_KV_HEADS, HEAD_DIM, w_rows), (0, 3, 1, 2)))
        outs[5].append(jnp.transpose(state_t, (1, 0, 2)))

    return (xp.reshape(bsz, seq, d), xs.reshape(db, t_new, d)) + tuple(jnp.stack(o) for o in outs)
```
